```python
import math
import jax, jax.numpy as jnp
from jax import lax
import numpy as np

D_MODEL = 4096
BATCH = 1
SEQ = 8192
DEPTH = 4

GRID_W = 64
CTX_LEN = 256
MIX_WIDTH = D_MODEL
RET_WIDTH = MIX_WIDTH // 2
CONV_WIDTH = MIX_WIDTH - RET_WIDTH
RET_HEADS = 8
RET_HEAD_DIM = RET_WIDTH // RET_HEADS
CONV_K = 3
CHUNK = 128
D_FF = 4 * D_MODEL
ADA_RANK = D_MODEL // 8
N_MOD = 6
IN_COLS = 4 * RET_WIDTH + 3 * CONV_WIDTH
SPLITS = [RET_WIDTH, 2 * RET_WIDTH, 3 * RET_WIDTH, 4 * RET_WIDTH,
          4 * RET_WIDTH + CONV_WIDTH, 4 * RET_WIDTH + 2 * CONV_WIDTH]
ROPE_THETA = 10000.0
EPS = 1e-6

kernel_name = "hybrid_retention_shortconv_dit"


def rms_norm(x, w):
    xf = x.astype(jnp.float32)
    y = xf * lax.rsqrt(jnp.mean(xf * xf, axis=-1, keepdims=True) + EPS)
    return (y * w.astype(jnp.float32)).astype(x.dtype)


def modulate(h, shift, scale):
    return h * (1.0 + scale) + shift


def ada_mod(s, w_down, w_up, b):
    m = (s @ w_down) @ w_up + b
    return [t[:, None, :] for t in jnp.split(m, N_MOD, axis=-1)]


def rope_2d_tables(n_tokens):
    rows = n_tokens // GRID_W
    row = jnp.broadcast_to(jnp.arange(rows, dtype=jnp.float32)[:, None], (rows, GRID_W)).reshape(-1)
    col = jnp.broadcast_to(jnp.arange(GRID_W, dtype=jnp.float32)[None, :], (rows, GRID_W)).reshape(-1)
    nf = RET_HEAD_DIM // 4
    inv = ROPE_THETA ** (-jnp.arange(nf, dtype=jnp.float32) / nf)
    ang = jnp.concatenate([row[:, None] * inv, col[:, None] * inv], axis=-1)
    return jnp.cos(ang), jnp.sin(ang)


def apply_rope(t, cos, sin):
    half = t.shape[-1] // 2
    t1, t2 = t[..., :half], t[..., half:]
    return jnp.concatenate([t1 * cos - t2 * sin, t1 * sin + t2 * cos], axis=-1)


def to_heads(t):
    b, l, _ = t.shape
    return t.reshape(b, l, RET_HEADS, RET_HEAD_DIM).transpose(0, 2, 1, 3).astype(jnp.float32)


def chunk_retention(q, k, v, log_gamma, s0, inclusive):
    b, h, L, dk = q.shape
    n = L // CHUNK
    qc = q.reshape(b, h, n, CHUNK, dk)
    kc = k.reshape(b, h, n, CHUNK, dk)
    vc = v.reshape(b, h, n, CHUNK, v.shape[-1])
    pos = jnp.arange(CHUNK, dtype=jnp.float32)
    diff = pos[:, None] - pos[None, :]
    keep = (diff >= 0) if inclusive else (diff > 0)
    lg = log_gamma[:, None, None]
    decay = jnp.where(keep[None], jnp.exp(lg * jnp.maximum(diff, 0.0)[None]), 0.0)
    scores = jnp.einsum('bhncd,bhnmd->bhncm', qc, kc) * decay[None, :, None]
    inner = jnp.einsum('bhncm,bhnme->bhnce', scores, vc)
    q_decay = jnp.exp(log_gamma[:, None] * (pos + 1.0))
    k_decay = jnp.exp(log_gamma[:, None] * (CHUNK - 1.0 - pos))
    chunk_kv = jnp.einsum('bhnmd,hm,bhnme->bhnde', kc, k_decay, vc)
    chunk_gamma = jnp.exp(log_gamma * CHUNK)[None, :, None, None]

    def step(s, inp):
        q_n, kv_n = inp
        cross = jnp.einsum('bhcd,hc,bhde->bhce', q_n, q_decay, s)
        return chunk_gamma * s + kv_n, cross

    s_fin, cross = lax.scan(step, s0, (jnp.moveaxis(qc, 2, 0), jnp.moveaxis(chunk_kv, 2, 0)))
    out = inner + jnp.moveaxis(cross, 0, 2)
    return out.reshape(b, h, L, -1), s_fin


def short_conv(u, w):
    L = u.shape[1]
    up = jnp.pad(u, ((0, 0), (1, 1), (0, 0)))
    return up[:, :L] * w[:, 0] + up[:, 1:L + 1] * w[:, 1] + up[:, 2:] * w[:, 2]


def merge_groups(ret, g, bg, cg, xc, ret_norm_w, conv_w, w_out, dtype):
    b, _, L, _ = ret.shape
    r = ret.transpose(0, 2, 1, 3)
    r = r * lax.rsqrt(jnp.mean(r * r, axis=-1, keepdims=True) + EPS)
    r = r.reshape(b, L, RET_WIDTH) * ret_norm_w.astype(jnp.float32)
    ret_out = (jax.nn.silu(g.astype(jnp.float32)) * r).astype(dtype)
    conv_out = bg * short_conv(cg * xc, conv_w)
    return jnp.concatenate([ret_out, conv_out.astype(dtype)], axis=-1) @ w_out


def mix_layer(h_lat, h_ctx, w_in, w_out, conv_w, ret_norm_w, p_fwd, p_bwd, ctx_out):
    b, n_lat, _ = h_lat.shape
    dt = h_lat.dtype
    lat = jnp.split(h_lat @ w_in, SPLITS, axis=-1)
    ctx = jnp.split(h_ctx @ w_in, SPLITS, axis=-1)
    lg_f = jax.nn.log_sigmoid(p_fwd.astype(jnp.float32))
    lg_b = jax.nn.log_sigmoid(p_bwd.astype(jnp.float32))
    qscale = RET_HEAD_DIM ** -0.5
    cos, sin = rope_2d_tables(n_lat)
    ql = apply_rope(to_heads(lat[0]), cos, sin) * qscale
    kl = apply_rope(to_heads(lat[1]), cos, sin)
    vl = to_heads(lat[2])
    qc = to_heads(ctx[0]) * qscale
    kc = to_heads(ctx[1])
    vc = to_heads(ctx[2])
    flip = lambda t: jnp.flip(t, axis=2)
    zero = jnp.zeros((b, RET_HEADS, RET_HEAD_DIM, RET_HEAD_DIM), jnp.float32)
    oc_f, sc_f = chunk_retention(qc, kc, vc, lg_f, zero, True)
    oc_b, sc_b = chunk_retention(flip(qc), flip(kc), flip(vc), lg_b, zero, False)
    ol_f, _ = chunk_retention(ql, kl, vl, lg_f, sc_f, True)
    ol_b, _ = chunk_retention(flip(ql), flip(kl), flip(vl), lg_b, sc_b, False)
    y_lat = merge_groups(ol_f + flip(ol_b), lat[3], lat[4], lat[5], lat[6],
                         ret_norm_w, conv_w, w_out, dt)
    y_ctx = None
    if ctx_out:
        y_ctx = merge_groups(oc_f + flip(oc_b), ctx[3], ctx[4], ctx[5], ctx[6],
                             ret_norm_w, conv_w, w_out, dt)
    return y_lat, y_ctx


def sq_relu_mlp(h, w1, w2):
    z = jax.nn.relu(h @ w1)
    return (z * z) @ w2


def setup_inputs(seed: int = 0) -> dict:
    key = jax.random.key(seed)
    ks = jax.random.split(key, 18)
    f32 = jnp.float32
    nrm = lambda k, shape, s: jax.random.normal(k, shape, f32) * s
    a = 5.0 + jnp.arange(RET_HEADS, dtype=f32)
    decay_base = jnp.log(jnp.exp2(a) - 1.0)
    return {
        "x": nrm(ks[0], (BATCH, SEQ, D_MODEL), 1.0),
        "c": nrm(ks[1], (BATCH, D_MODEL), 1.0),
        "ctx": nrm(ks[2], (BATCH, CTX_LEN, D_MODEL), 1.0),
        "c_ctx": nrm(ks[3], (D_MODEL,), 1.0),
        "ada_w_down": nrm(ks[4], (DEPTH, D_MODEL, ADA_RANK), D_MODEL ** -0.5),
        "ada_w_up": nrm(ks[5], (DEPTH, ADA_RANK, N_MOD * D_MODEL), 0.2 * ADA_RANK ** -0.5),
        "ada_b": nrm(ks[6], (DEPTH, N_MOD * D_MODEL), 0.02),
        "norm1_w": 1.0 + nrm(ks[7], (DEPTH, D_MODEL), 0.02),
        "norm2_w": 1.0 + nrm(ks[8], (DEPTH, D_MODEL), 0.02),
        "w_in": nrm(ks[9], (DEPTH, D_MODEL, IN_COLS), D_MODEL ** -0.5),
        "ret_decay_fwd": decay_base + nrm(ks[10], (DEPTH, RET_HEADS), 0.05),
        "ret_decay_bwd": decay_base + nrm(ks[11], (DEPTH, RET_HEADS), 0.05),
        "ret_norm_w": 1.0 + nrm(ks[12], (DEPTH, RET_WIDTH), 0.02),
        "conv_w": nrm(ks[13], (DEPTH, CONV_WIDTH, CONV_K), CONV_K ** -0.5),
        "w_out": nrm(ks[14], (DEPTH, MIX_WIDTH, D_MODEL), MIX_WIDTH ** -0.5),
        "mlp_w1": nrm(ks[15], (DEPTH, D_MODEL, D_FF), D_MODEL ** -0.5),
        "mlp_w2": nrm(ks[16], (DEPTH, D_FF, D_MODEL), D_FF ** -0.5),
        "final_norm_w": 1.0 + nrm(ks[17], (D_MODEL,), 0.02),
    }


def reference(x, c, ctx, c_ctx, ada_w_down, ada_w_up, ada_b, norm1_w, norm2_w, w_in,
              ret_decay_fwd, ret_decay_bwd, ret_norm_w, conv_w, w_out, mlp_w1, mlp_w2,
              final_norm_w):
    s_lat = jax.nn.silu(c)
    s_ctx = jax.nn.silu(c_ctx)[None, :]
    h = x
    hc = ctx
    for l in range(DEPTH):
        last = l == DEPTH - 1
        sh1, sc1, g1, sh2, sc2, g2 = ada_mod(s_lat, ada_w_down[l], ada_w_up[l], ada_b[l])
        csh1, csc1, cg1, csh2, csc2, cg2 = ada_mod(s_ctx, ada_w_down[l], ada_w_up[l], ada_b[l])
        a_lat = modulate(rms_norm(h, norm1_w[l]), sh1, sc1)
        a_ctx = modulate(rms_norm(hc, norm1_w[l]), csh1, csc1)
        y_lat, y_ctx = mix_layer(a_lat, a_ctx, w_in[l], w_out[l], conv_w[l], ret_norm_w[l],
                                 ret_decay_fwd[l], ret_decay_bwd[l], not last)
        h = h + g1 * y_lat
        m_lat = modulate(rms_norm(h, norm2_w[l]), sh2, sc2)
        h = h + g2 * sq_relu_mlp(m_lat, mlp_w1[l], mlp_w2[l])
        if not last:
            hc = hc + cg1 * y_ctx
            m_ctx = modulate(rms_norm(hc, norm2_w[l]), csh2, csc2)
            hc = hc + cg2 * sq_relu_mlp(m_ctx, mlp_w1[l], mlp_w2[l])
    return rms_norm(h, final_norm_w)
```

```python
import functools

import jax
import jax.numpy as jnp
from jax import lax
from jax.experimental import pallas as pl
from jax.experimental.pallas import tpu as pltpu

GRID_W = 64
CHUNK = 128
ROPE_THETA = 10000.0
EPS = 1e-6
N_MOD = 6
ADA_ROWS = 8
V7X_VMEM_LIMIT_BYTES = 60 * 1024 * 1024

_bf16 = jnp.bfloat16
_f32 = jnp.float32


def _block(total, target, align):
    best = None
    for d in range(align, min(total, target) + 1, align):
        if total % d == 0:
            best = d
    if best is None:
        raise ValueError(f"no block for {total} (target {target}, align {align})")
    return best


def _params(sem):
    return pltpu.CompilerParams(dimension_semantics=sem, vmem_limit_bytes=V7X_VMEM_LIMIT_BYTES)


def _ada_down_kernel(s_ref, wd_ref, t_ref):
    t_ref[0] = jnp.dot(s_ref[...].astype(_bf16), wd_ref[0].astype(_bf16),
                       preferred_element_type=_f32)


def _ada_up_kernel(t_ref, wu_ref, b_ref, m_ref):
    m_ref[0] = jnp.dot(t_ref[0].astype(_bf16), wu_ref[0].astype(_bf16),
                       preferred_element_type=_f32) + b_ref[0]


def _ada_mod(s, w_down, w_up, b):
    depth, d, rank = w_down.shape
    n = w_up.shape[2]
    t = pl.pallas_call(
        _ada_down_kernel,
        out_shape=jax.ShapeDtypeStruct((depth, ADA_ROWS, rank), _f32),
        grid=(depth,),
        in_specs=[pl.BlockSpec((ADA_ROWS, d), lambda l: (0, 0)),
                  pl.BlockSpec((1, d, rank), lambda l: (l, 0, 0))],
        out_specs=pl.BlockSpec((1, ADA_ROWS, rank), lambda l: (l, 0, 0)),
        compiler_params=_params(("arbitrary",)),
        name="ada_down",
    )(s, w_down)
    bn = _block(n, 4096, 128)
    return pl.pallas_call(
        _ada_up_kernel,
        out_shape=jax.ShapeDtypeStruct((depth, ADA_ROWS, n), _f32),
        grid=(depth, n // bn),
        in_specs=[pl.BlockSpec((1, ADA_ROWS, rank), lambda l, j: (l, 0, 0)),
                  pl.BlockSpec((1, rank, bn), lambda l, j: (l, 0, j)),
                  pl.BlockSpec((1, 1, bn), lambda l, j: (l, 0, j))],
        out_specs=pl.BlockSpec((1, ADA_ROWS, bn), lambda l, j: (l, 0, j)),
        compiler_params=_params(("arbitrary", "arbitrary")),
        name="ada_up",
    )(t, w_up, b.reshape(depth, 1, n))


def _row_select(mod_ref, row0, rows, n_ctx):
    ridx = row0 + lax.broadcasted_iota(jnp.int32, (rows, 1), 0)
    return jnp.where(ridx < n_ctx, mod_ref[1:2, :], mod_ref[0:1, :])


def _norm_mod_kernel(x_ref, w_ref, sh_ref, sc_ref, o_ref, *, n_ctx):
    rows = x_ref.shape[0]
    x = x_ref[...]
    y = x * lax.rsqrt(jnp.mean(x * x, axis=-1, keepdims=True) + EPS) * w_ref[...]
    row0 = pl.program_id(0) * rows
    sh = _row_select(sh_ref, row0, rows, n_ctx)
    sc = _row_select(sc_ref, row0, rows, n_ctx)
    o_ref[...] = (y * (1.0 + sc) + sh).astype(o_ref.dtype)


def _norm_mod(h, w, sh, sc, n_ctx):
    m, d = h.shape
    br = _block(m, 512, 8)
    return pl.pallas_call(
        functools.partial(_norm_mod_kernel, n_ctx=n_ctx),
        out_shape=jax.ShapeDtypeStruct((m, d), _bf16),
        grid=(m // br,),
        in_specs=[pl.BlockSpec((br, d), lambda i: (i, 0)),
                  pl.BlockSpec((1, d), lambda i: (0, 0)),
                  pl.BlockSpec((ADA_ROWS, d), lambda i: (0, 0)),
                  pl.BlockSpec((ADA_ROWS, d), lambda i: (0, 0))],
        out_specs=pl.BlockSpec((br, d), lambda i: (i, 0)),
        compiler_params=_params(("arbitrary",)),
        name="norm_mod",
    )(h, w.reshape(1, d), sh, sc)


def _final_norm_kernel(x_ref, w_ref, o_ref):
    x = x_ref[...]
    o_ref[...] = x * lax.rsqrt(jnp.mean(x * x, axis=-1, keepdims=True) + EPS) * w_ref[...]


def _final_norm(h, w, n_ctx):
    m, d = h.shape
    br = _block(n_ctx, 512, 8)
    skip = n_ctx // br
    return pl.pallas_call(
        _final_norm_kernel,
        out_shape=jax.ShapeDtypeStruct((m - n_ctx, d), _f32),
        grid=((m - n_ctx) // br,),
        in_specs=[pl.BlockSpec((br, d), lambda i: (i + skip, 0)),
                  pl.BlockSpec((1, d), lambda i: (0, 0))],
        out_specs=pl.BlockSpec((br, d), lambda i: (i, 0)),
        compiler_params=_params(("arbitrary",)),
        name="final_norm",
    )(h, w.reshape(1, d))


def _in_proj_kernel(a_ref, w_ref, cos_ref, sin_ref, o_ref, acc_ref, *, n_rope_blocks, n_q_blocks,
                    head_dim):
    acc_ref[...] = jnp.dot(a_ref[...], w_ref[...], preferred_element_type=_f32)
    j = pl.program_id(0)
    half = head_dim // 2

    @pl.when(j < n_rope_blocks)
    def _():
        cos = cos_ref[...]
        sin = sin_ref[...]
        scale = jnp.where(j < n_q_blocks, head_dim ** -0.5, 1.0).astype(_f32)
        for hd in range(acc_ref.shape[1] // head_dim):
            c0 = hd * head_dim
            t1 = acc_ref[:, c0:c0 + half]
            t2 = acc_ref[:, c0 + half:c0 + head_dim]
            o_ref[:, c0:c0 + half] = ((t1 * cos - t2 * sin) * scale).astype(o_ref.dtype)
            o_ref[:, c0 + half:c0 + head_dim] = ((t1 * sin + t2 * cos) * scale).astype(o_ref.dtype)

    @pl.when(j >= n_rope_blocks)
    def _():
        o_ref[...] = acc_ref[...].astype(o_ref.dtype)


def _in_proj(a, w, cos, sin, ret_width, head_dim):
    m, k = a.shape
    n = w.shape[1]
    bm = _block(m, 1056, 16)
    bn = _block(ret_width, 1024, head_dim)
    half = head_dim // 2
    return pl.pallas_call(
        functools.partial(_in_proj_kernel, n_rope_blocks=2 * ret_width // bn,
                          n_q_blocks=ret_width // bn, head_dim=head_dim),
        out_shape=jax.ShapeDtypeStruct((m, n), _bf16),
        grid=(n // bn, m // bm),
        in_specs=[pl.BlockSpec((bm, k), lambda j, i: (i, 0)),
                  pl.BlockSpec((k, bn), lambda j, i: (0, j)),
                  pl.BlockSpec((bm, half), lambda j, i: (i, 0)),
                  pl.BlockSpec((bm, half), lambda j, i: (i, 0))],
        out_specs=pl.BlockSpec((bm, bn), lambda j, i: (i, j)),
        scratch_shapes=[pltpu.VMEM((bm, bn), _f32)],
        compiler_params=_params(("arbitrary", "arbitrary")),
        name="in_proj",
    )(a, w, cos, sin)


def _retention_kernel(lg_ref, q_ref, k_ref, v_ref, g_ref, w_ref, o_ref, acc_ref, s_ref, *,
                      n_ctx_chunks):
    head = pl.program_id(0)
    n_chunks = q_ref.shape[0] // CHUNK
    lg_f = lg_ref[0, head]
    lg_b = lg_ref[1, head]

    ii = lax.broadcasted_iota(jnp.int32, (CHUNK, CHUNK), 0)
    jj = lax.broadcasted_iota(jnp.int32, (CHUNK, CHUNK), 1)
    diff = (ii - jj).astype(_f32)
    decay = jnp.where(diff >= 0.0, jnp.exp(lg_f * jnp.maximum(diff, 0.0)),
                      jnp.exp(lg_b * jnp.maximum(-diff, 0.0)))
    pos = lax.broadcasted_iota(jnp.int32, (CHUNK, 1), 0).astype(_f32)
    chunk_len = jnp.full((1, 1), float(CHUNK), _f32)
    q_decay_f = jnp.exp(lg_f * (pos + 1.0))
    k_decay_f = jnp.exp(lg_f * (CHUNK - 1.0 - pos))
    gamma_f = jnp.exp(lg_f * chunk_len)
    q_decay_b = jnp.exp(lg_b * (CHUNK - pos))
    k_decay_b = jnp.exp(lg_b * pos)
    gamma_b = jnp.exp(lg_b * chunk_len)

    def scaled(t, d):
        return (t.astype(_f32) * d).astype(_bf16)

    def kv_outer(kc, vc, d):
        return lax.dot_general(scaled(kc, d), vc, (((0,), (0,)), ((), ())),
                               preferred_element_type=_f32)

    s_ref[...] = jnp.zeros_like(s_ref)

    def fwd(c, carry):
        rows = pl.ds(pl.multiple_of(c * CHUNK, CHUNK), CHUNK)
        qc, kc, vc = q_ref[rows, :], k_ref[rows, :], v_ref[rows, :]
        s = lax.dot_general(qc, kc, (((1,), (1,)), ((), ())), preferred_element_type=_f32)
        o = jnp.dot((s * decay).astype(_bf16), vc, preferred_element_type=_f32)
        o += jnp.dot(scaled(qc, q_decay_f), s_ref[...].astype(_bf16), preferred_element_type=_f32)
        acc_ref[rows, :] = o
        s_ref[...] = gamma_f * s_ref[...] + kv_outer(kc, vc, k_decay_f)
        return carry

    lax.fori_loop(0, n_chunks, fwd, 0)

    s_ref[...] = jnp.zeros_like(s_ref)
    w = w_ref[...]

    def bwd(t, carry):
        c = jnp.where(t < n_ctx_chunks, n_ctx_chunks - 1 - t, n_chunks - 1 + n_ctx_chunks - t)
        rows = pl.ds(pl.multiple_of(c * CHUNK, CHUNK), CHUNK)
        qc, kc, vc = q_ref[rows, :], k_ref[rows, :], v_ref[rows, :]
        o = acc_ref[rows, :] + jnp.dot(scaled(qc, q_decay_b), s_ref[...].astype(_bf16),
                                       preferred_element_type=_f32)
        s_ref[...] = gamma_b * s_ref[...] + kv_outer(kc, vc, k_decay_b)
        r = o * lax.rsqrt(jnp.mean(o * o, axis=-1, keepdims=True) + EPS) * w
        g = g_ref[rows, :].astype(_f32)
        o_ref[rows, :] = (g * (1.0 / (1.0 + jnp.exp(-g))) * r).astype(o_ref.dtype)
        return carry

    lax.fori_loop(0, n_chunks, bwd, 0)


def _retention(y, lg, norm_w, n_ctx, heads, head_dim):
    m = y.shape[0]
    ret_width = heads * head_dim
    blk = lambda part: pl.BlockSpec((m, head_dim), lambda h: (0, part * heads + h))
    return pl.pallas_call(
        functools.partial(_retention_kernel, n_ctx_chunks=n_ctx // CHUNK),
        out_shape=jax.ShapeDtypeStruct((m, ret_width), _bf16),
        grid=(heads,),
        in_specs=[pl.BlockSpec(memory_space=pltpu.SMEM),
                  blk(0), blk(1), blk(2), blk(3),
                  pl.BlockSpec((1, head_dim), lambda h: (0, h))],
        out_specs=pl.BlockSpec((m, head_dim), lambda h: (0, h)),
        scratch_shapes=[pltpu.VMEM((m, head_dim), _f32),
                        pltpu.VMEM((head_dim, head_dim), _f32)],
        compiler_params=_params(("arbitrary",)),
        name="retention",
    )(lg, y, y, y, y, norm_w.reshape(1, ret_width))


def _conv_kernel(bg_ref, cg_ref, xc_ref, w_ref, o_ref, *, n_ctx, rows_per_step):
    m = o_ref.shape[0]
    r = rows_per_step
    w0, w1, w2 = w_ref[0:1, :], w_ref[1:2, :], w_ref[2:3, :]
    ridx = lax.broadcasted_iota(jnp.int32, (r, 1), 0)

    def u_rows(start, size):
        rows = pl.ds(pl.multiple_of(start, 16), size)
        return cg_ref[rows, :].astype(_f32) * xc_ref[rows, :].astype(_f32)

    def step(i, prev_last):
        r0 = i * r
        u = u_rows(r0, r)
        nxt_start = jnp.minimum(r0 + r, m - 16)
        nxt_first = u_rows(nxt_start, 16)[0:1, :]
        g = r0 + ridx
        is_start = (g == 0) | (g == n_ctx)
        is_end = (g == n_ctx - 1) | (g == m - 1)
        up = jnp.where(ridx == 0, prev_last, pltpu.roll(u, 1, axis=0))
        up = jnp.where(is_start, 0.0, up)
        dn = jnp.where(ridx == r - 1, nxt_first, pltpu.roll(u, r - 1, axis=0))
        dn = jnp.where(is_end, 0.0, dn)
        conv = up * w0 + u * w1 + dn * w2
        rows = pl.ds(pl.multiple_of(r0, 16), r)
        o_ref[rows, :] = (bg_ref[rows, :].astype(_f32) * conv).astype(o_ref.dtype)
        return u[r - 1:r, :]

    lax.fori_loop(0, m // r, step, jnp.zeros((1, o_ref.shape[1]), _f32))


def _gated_conv(y, conv_w_t, n_ctx, ret_width, conv_width):
    m = y.shape[0]
    bc = _block(conv_width, 256, 128)
    nb = conv_width // bc
    base = 4 * ret_width // bc
    r = _block(n_ctx, 128, 16)
    blk = lambda part: pl.BlockSpec((m, bc), lambda j: (0, base + part * nb + j))
    return pl.pallas_call(
        functools.partial(_conv_kernel, n_ctx=n_ctx, rows_per_step=r),
        out_shape=jax.ShapeDtypeStruct((m, conv_width), _bf16),
        grid=(nb,),
        in_specs=[blk(0), blk(1), blk(2), pl.BlockSpec((3, bc), lambda j: (0, j))],
        out_specs=pl.BlockSpec((m, bc), lambda j: (0, j)),
        compiler_params=_params(("arbitrary",)),
        name="gated_conv",
    )(y, y, y, conv_w_t)


def _out_proj_kernel(r_ref, c_ref, wr_ref, wc_ref, h_ref, g_ref, o_ref, *, n_ctx):
    rows = r_ref.shape[0]
    y = (jnp.dot(r_ref[...], wr_ref[...], preferred_element_type=_f32)
         + jnp.dot(c_ref[...], wc_ref[...], preferred_element_type=_f32))
    gate = _row_select(g_ref, pl.program_id(1) * rows, rows, n_ctx)
    o_ref[...] = h_ref[...] + gate * y


def _out_proj(ret, conv, w, h, gate, n_ctx):
    m, kr = ret.shape
    kc = conv.shape[1]
    n = w.shape[1]
    bm = _block(m, 1056, 16)
    bn = _block(n, 1024, 128)
    return pl.pallas_call(
        functools.partial(_out_proj_kernel, n_ctx=n_ctx),
        out_shape=jax.ShapeDtypeStruct((m, n), _f32),
        grid=(n // bn, m // bm),
        in_specs=[pl.BlockSpec((bm, kr), lambda j, i: (i, 0)),
                  pl.BlockSpec((bm, kc), lambda j, i: (i, 0)),
                  pl.BlockSpec((kr, bn), lambda j, i: (0, j)),
                  pl.BlockSpec((kc, bn), lambda j, i: (kr // kc, j)),
                  pl.BlockSpec((bm, bn), lambda j, i: (i, j)),
                  pl.BlockSpec((ADA_ROWS, bn), lambda j, i: (0, j))],
        out_specs=pl.BlockSpec((bm, bn), lambda j, i: (i, j)),
        compiler_params=_params(("arbitrary", "arbitrary")),
        name="out_proj",
    )(ret, conv, w, w, h, gate)


def _mlp_up_kernel(a_ref, w_ref, o_ref):
    z = jnp.maximum(jnp.dot(a_ref[...], w_ref[...], preferred_element_type=_f32), 0.0)
    o_ref[...] = (z * z).astype(o_ref.dtype)


def _mlp_up(a, w):
    m, k = a.shape
    n = w.shape[1]
    bm = _block(m, 1056, 16)
    bn = _block(n, 1024, 128)
    return pl.pallas_call(
        _mlp_up_kernel,
        out_shape=jax.ShapeDtypeStruct((m, n), _bf16),
        grid=(n // bn, m // bm),
        in_specs=[pl.BlockSpec((bm, k), lambda j, i: (i, 0)),
                  pl.BlockSpec((k, bn), lambda j, i: (0, j))],
        out_specs=pl.BlockSpec((bm, bn), lambda j, i: (i, j)),
        compiler_params=_params(("arbitrary", "arbitrary")),
        name="mlp_up",
    )(a, w)


def _mlp_down_kernel(z_ref, w_ref, h_ref, g_ref, o_ref, acc_ref, *, n_ctx):
    kk = pl.program_id(2)
    rows = z_ref.shape[0]
    part = jnp.dot(z_ref[...], w_ref[...], preferred_element_type=_f32)

    @pl.when(kk == 0)
    def _():
        acc_ref[...] = part

    @pl.when(kk > 0)
    def _():
        acc_ref[...] += part

    @pl.when(kk == pl.num_programs(2) - 1)
    def _():
        gate = _row_select(g_ref, pl.program_id(1) * rows, rows, n_ctx)
        o_ref[...] = h_ref[...] + gate * acc_ref[...]


def _mlp_down(z, w, h, gate, n_ctx):
    m, k = z.shape
    n = w.shape[1]
    bm = _block(m, 1056, 16)
    bn = _block(n, 1024, 128)
    bk = _block(k, 2048, 256)
    return pl.pallas_call(
        functools.partial(_mlp_down_kernel, n_ctx=n_ctx),
        out_shape=jax.ShapeDtypeStruct((m, n), _f32),
        grid=(n // bn, m // bm, k // bk),
        in_specs=[pl.BlockSpec((bm, bk), lambda j, i, kk: (i, kk)),
                  pl.BlockSpec((bk, bn), lambda j, i, kk: (kk, j)),
                  pl.BlockSpec((bm, bn), lambda j, i, kk: (i, j)),
                  pl.BlockSpec((ADA_ROWS, bn), lambda j, i, kk: (0, j))],
        out_specs=pl.BlockSpec((bm, bn), lambda j, i, kk: (i, j)),
        scratch_shapes=[pltpu.VMEM((bm, bn), _f32)],
        compiler_params=_params(("arbitrary", "arbitrary", "arbitrary")),
        name="mlp_down",
    )(z, w, h, gate)


def _rope_tables(n_lat, n_ctx, head_dim):
    rows = n_lat // GRID_W
    row = jnp.broadcast_to(jnp.arange(rows, dtype=_f32)[:, None], (rows, GRID_W)).reshape(-1)
    col = jnp.broadcast_to(jnp.arange(GRID_W, dtype=_f32)[None, :], (rows, GRID_W)).reshape(-1)
    nf = head_dim // 4
    inv = ROPE_THETA ** (-jnp.arange(nf, dtype=_f32) / nf)
    ang = jnp.concatenate([row[:, None] * inv, col[:, None] * inv], axis=-1)
    cos = jnp.concatenate([jnp.ones((n_ctx, 2 * nf), _f32), jnp.cos(ang)], axis=0)
    sin = jnp.concatenate([jnp.zeros((n_ctx, 2 * nf), _f32), jnp.sin(ang)], axis=0)
    return cos, sin


def kernel(x, c, ctx, c_ctx, ada_w_down, ada_w_up, ada_b, norm1_w, norm2_w, w_in, ret_decay_fwd,
           ret_decay_bwd, ret_norm_w, conv_w, w_out, mlp_w1, mlp_w2, final_norm_w):
    batch, n_lat, d = x.shape
    n_ctx = ctx.shape[1]
    depth = w_in.shape[0]
    heads = ret_decay_fwd.shape[1]
    ret_width = ret_norm_w.shape[1]
    conv_width = conv_w.shape[1]
    head_dim = ret_width // heads
    assert batch == 1 and c.shape[0] == 1
    assert n_ctx % CHUNK == 0 and n_lat % CHUNK == 0 and n_lat % GRID_W == 0
    assert w_in.shape[2] == 4 * ret_width + 3 * conv_width and ret_width + conv_width == d

    s = jnp.zeros((ADA_ROWS, d), _f32).at[0].set(jax.nn.silu(c[0])).at[1].set(jax.nn.silu(c_ctx))
    mods = _ada_mod(s, ada_w_down, ada_w_up, ada_b)
    cos, sin = _rope_tables(n_lat, n_ctx, head_dim)
    lg = jnp.stack([jax.nn.log_sigmoid(ret_decay_fwd.astype(_f32)),
                    jax.nn.log_sigmoid(ret_decay_bwd.astype(_f32))], axis=1)

    h = jnp.concatenate([ctx[0], x[0]], axis=0)
    for l in range(depth):
        sh1, sc1, g1, sh2, sc2, g2 = [mods[l, :, j * d:(j + 1) * d] for j in range(N_MOD)]
        a = _norm_mod(h, norm1_w[l], sh1, sc1, n_ctx)
        y = _in_proj(a, w_in[l].astype(_bf16), cos, sin, ret_width, head_dim)
        ret = _retention(y, lg[l], ret_norm_w[l], n_ctx, heads, head_dim)
        conv = _gated_conv(y, conv_w[l].T, n_ctx, ret_width, conv_width)
        h = _out_proj(ret, conv, w_out[l].astype(_bf16), h, g1, n_ctx)
        a = _norm_mod(h, norm2_w[l], sh2, sc2, n_ctx)
        z = _mlp_up(a, mlp_w1[l].astype(_bf16))
        h = _mlp_down(z, mlp_w2[l].astype(_bf16), h, g2, n_ctx)
    return _final_norm(h, final_norm_w, n_ctx)[None]
```

```python
import functools

import jax
import jax.numpy as jnp
from jax import lax
from jax.experimental import pallas as pl
from jax.experimental.pallas import tpu as pltpu

GRID_W = 64
CHUNK = 128
ROPE_THETA = 10000.0
EPS = 1e-6
N_MOD = 6
ADA_ROWS = 8
NORM_GROUP = 16
CAST_ROWS = 256
V7X_VMEM_LIMIT_BYTES = 60 * 1024 * 1024

_bf16 = jnp.bfloat16
_f32 = jnp.float32


def _block(total, target, align):
    best = None
    for d in range(align, min(total, target) + 1, align):
        if total % d == 0:
            best = d
    if best is None:
        raise ValueError(f"no block for {total} (target {target}, align {align})")
    return best


def _params(sem):
    return pltpu.CompilerParams(dimension_semantics=sem, vmem_limit_bytes=V7X_VMEM_LIMIT_BYTES)


def _fetch_weight_block(w_hbm, stage_ref, wb_ref, sem, layer, t, n_blocks, origin):
    rows, cols = stage_ref.shape

    def aligned(v, a):
        return v if isinstance(v, int) else pl.multiple_of(v, a)

    def copy(tt):
        r0, c0 = origin(tt)
        src = w_hbm.at[layer, pl.ds(aligned(r0, rows), rows), pl.ds(aligned(c0, cols), cols)]
        return pltpu.make_async_copy(src, stage_ref, sem.at[0])

    @pl.when(t == 0)
    def _():
        copy(t).start()

    copy(t).wait()

    def cast(r, carry):
        sl = pl.ds(pl.multiple_of(r * CAST_ROWS, CAST_ROWS), CAST_ROWS)
        wb_ref[sl, :] = stage_ref[sl, :].astype(_bf16)
        return carry

    lax.fori_loop(0, rows // CAST_ROWS, cast, 0)

    @pl.when(t + 1 < n_blocks)
    def _():
        copy(t + 1).start()


def _weight_scratch(bk, bn):
    assert bk % CAST_ROWS == 0
    return [pltpu.VMEM((bk, bn), _f32), pltpu.VMEM((bk, bn), _bf16), pltpu.SemaphoreType.DMA((1,))]


def _ada_down_kernel(s_ref, wd_ref, t_ref):
    t_ref[0] = jnp.dot(s_ref[...].astype(_bf16), wd_ref[0].astype(_bf16),
                       preferred_element_type=_f32)


def _ada_up_kernel(t_ref, wu_ref, b_ref, m_ref):
    m_ref[0] = jnp.dot(t_ref[0].astype(_bf16), wu_ref[0].astype(_bf16),
                       preferred_element_type=_f32) + b_ref[0]


def _ada_mod(s, w_down, w_up, b):
    depth, d, rank = w_down.shape
    n = w_up.shape[2]
    t = pl.pallas_call(
        _ada_down_kernel,
        out_shape=jax.ShapeDtypeStruct((depth, ADA_ROWS, rank), _f32),
        grid=(depth,),
        in_specs=[pl.BlockSpec((ADA_ROWS, d), lambda l: (0, 0)),
                  pl.BlockSpec((1, d, rank), lambda l: (l, 0, 0))],
        out_specs=pl.BlockSpec((1, ADA_ROWS, rank), lambda l: (l, 0, 0)),
        compiler_params=_params(("arbitrary",)),
        name="ada_down",
    )(s, w_down)
    bn = _block(n, 4096, 128)
    return pl.pallas_call(
        _ada_up_kernel,
        out_shape=jax.ShapeDtypeStruct((depth, ADA_ROWS, n), _f32),
        grid=(depth, n // bn),
        in_specs=[pl.BlockSpec((1, ADA_ROWS, rank), lambda l, j: (l, 0, 0)),
                  pl.BlockSpec((1, rank, bn), lambda l, j: (l, 0, j)),
                  pl.BlockSpec((1, 1, bn), lambda l, j: (l, 0, j))],
        out_specs=pl.BlockSpec((1, ADA_ROWS, bn), lambda l, j: (l, 0, j)),
        compiler_params=_params(("arbitrary", "arbitrary")),
        name="ada_up",
    )(t, w_up, b.reshape(depth, 1, n))


def _row_select(mod_ref, row0, rows, n_ctx):
    ridx = row0 + lax.broadcasted_iota(jnp.int32, (rows, 1), 0)
    return jnp.where(ridx < n_ctx, mod_ref[1:2, :], mod_ref[0:1, :])


def _norm_mod_kernel(x_ref, w_ref, sh_ref, sc_ref, o_ref, gain_ref, shift_ref, rstd_ref, *, n_ctx):
    rows, d = x_ref.shape
    g_rows = NORM_GROUP
    w = w_ref[...]
    for r in range(2):
        gain_ref[r * g_rows:(r + 1) * g_rows, :] = jnp.broadcast_to(
            w * (1.0 + sc_ref[r:r + 1, :]), (g_rows, d))
        shift_ref[r * g_rows:(r + 1) * g_rows, :] = jnp.broadcast_to(sh_ref[r:r + 1, :], (g_rows, d))

    def group(g):
        return pl.ds(pl.multiple_of(g * g_rows, g_rows), g_rows)

    def stats(g, carry):
        x = x_ref[group(g), :]
        rstd_ref[group(g), :] = lax.rsqrt(jnp.mean(x * x, axis=-1, keepdims=True) + EPS)
        return carry

    lax.fori_loop(0, rows // g_rows, stats, 0, unroll=4)

    ctx_groups = jnp.clip(n_ctx - pl.program_id(0) * rows, 0, rows) // g_rows

    def apply(g, carry):
        mod = pl.ds(pl.multiple_of(jnp.where(g < ctx_groups, g_rows, 0), g_rows), g_rows)
        x = x_ref[group(g), :]
        o_ref[group(g), :] = ((x * rstd_ref[group(g), :]) * gain_ref[mod, :]
                              + shift_ref[mod, :]).astype(o_ref.dtype)
        return carry

    lax.fori_loop(0, rows // g_rows, apply, 0, unroll=4)


def _norm_mod(h, w, sh, sc, n_ctx):
    m, d = h.shape
    assert n_ctx % NORM_GROUP == 0
    br = _block(m, 512, 4 * NORM_GROUP)
    return pl.pallas_call(
        functools.partial(_norm_mod_kernel, n_ctx=n_ctx),
        out_shape=jax.ShapeDtypeStruct((m, d), _bf16),
        grid=(m // br,),
        in_specs=[pl.BlockSpec((br, d), lambda i: (i, 0)),
                  pl.BlockSpec((1, d), lambda i: (0, 0)),
                  pl.BlockSpec((ADA_ROWS, d), lambda i: (0, 0)),
                  pl.BlockSpec((ADA_ROWS, d), lambda i: (0, 0))],
        out_specs=pl.BlockSpec((br, d), lambda i: (i, 0)),
        scratch_shapes=[pltpu.VMEM((2 * NORM_GROUP, d), _f32),
                        pltpu.VMEM((2 * NORM_GROUP, d), _f32),
                        pltpu.VMEM((br, 1), _f32)],
        compiler_params=_params(("arbitrary",)),
        name="norm_mod",
    )(h, w.reshape(1, d), sh, sc)


def _final_norm_kernel(x_ref, w_ref, o_ref):
    x = x_ref[...]
    o_ref[...] = x * lax.rsqrt(jnp.mean(x * x, axis=-1, keepdims=True) + EPS) * w_ref[...]


def _final_norm(h, w, n_ctx):
    m, d = h.shape
    br = _block(n_ctx, 512, 8)
    skip = n_ctx // br
    return pl.pallas_call(
        _final_norm_kernel,
        out_shape=jax.ShapeDtypeStruct((m - n_ctx, d), _f32),
        grid=((m - n_ctx) // br,),
        in_specs=[pl.BlockSpec((br, d), lambda i: (i + skip, 0)),
                  pl.BlockSpec((1, d), lambda i: (0, 0))],
        out_specs=pl.BlockSpec((br, d), lambda i: (i, 0)),
        compiler_params=_params(("arbitrary",)),
        name="final_norm",
    )(h, w.reshape(1, d))


def _in_proj_kernel(a_ref, w_hbm, cos_ref, sin_ref, o_ref, stage_ref, wb_ref, sem, acc_ref, *,
                    layer, n_rope_blocks, n_q_blocks, head_dim):
    j = pl.program_id(0)
    bn = wb_ref.shape[1]

    @pl.when(pl.program_id(1) == 0)
    def _():
        _fetch_weight_block(w_hbm, stage_ref, wb_ref, sem, layer, j, pl.num_programs(0),
                            lambda t: (0, t * bn))

    acc_ref[...] = jnp.dot(a_ref[...], wb_ref[...], preferred_element_type=_f32)
    half = head_dim // 2

    @pl.when(j < n_rope_blocks)
    def _():
        cos = cos_ref[...]
        sin = sin_ref[...]
        scale = jnp.where(j < n_q_blocks, head_dim ** -0.5, 1.0).astype(_f32)
        for hd in range(bn // head_dim):
            c0 = hd * head_dim
            t1 = acc_ref[:, c0:c0 + half]
            t2 = acc_ref[:, c0 + half:c0 + head_dim]
            o_ref[:, c0:c0 + half] = ((t1 * cos - t2 * sin) * scale).astype(o_ref.dtype)
            o_ref[:, c0 + half:c0 + head_dim] = ((t1 * sin + t2 * cos) * scale).astype(o_ref.dtype)

    @pl.when(j >= n_rope_blocks)
    def _():
        o_ref[...] = acc_ref[...].astype(o_ref.dtype)


def _in_proj(a, w, layer, cos, sin, ret_width, head_dim):
    m, k = a.shape
    n = w.shape[2]
    bm = _block(m, 1056, 16)
    bn = _block(ret_width, 1024, head_dim)
    assert n % bn == 0
    half = head_dim // 2
    return pl.pallas_call(
        functools.partial(_in_proj_kernel, layer=layer, n_rope_blocks=2 * ret_width // bn,
                          n_q_blocks=ret_width // bn, head_dim=head_dim),
        out_shape=jax.ShapeDtypeStruct((m, n), _bf16),
        grid=(n // bn, m // bm),
        in_specs=[pl.BlockSpec((bm, k), lambda j, i: (i, 0)),
                  pl.BlockSpec(memory_space=pl.ANY),
                  pl.BlockSpec((bm, half), lambda j, i: (i, 0)),
                  pl.BlockSpec((bm, half), lambda j, i: (i, 0))],
        out_specs=pl.BlockSpec((bm, bn), lambda j, i: (i, j)),
        scratch_shapes=_weight_scratch(k, bn) + [pltpu.VMEM((bm, bn), _f32)],
        compiler_params=_params(("arbitrary", "arbitrary")),
        name="in_proj",
    )(a, w, cos, sin)


def _retention_kernel(lg_ref, q_ref, k_ref, v_ref, g_ref, w_ref, o_ref, snap_ref, s_ref, *,
                      n_ctx_chunks, unroll):
    head = pl.program_id(0)
    n_chunks = q_ref.shape[0] // CHUNK
    lg_f = lg_ref[0, head]
    lg_b = lg_ref[1, head]

    ii = lax.broadcasted_iota(jnp.int32, (CHUNK, CHUNK), 0)
    jj = lax.broadcasted_iota(jnp.int32, (CHUNK, CHUNK), 1)
    diff = (ii - jj).astype(_f32)
    decay = jnp.where(diff >= 0.0, jnp.exp(lg_f * jnp.maximum(diff, 0.0)),
                      jnp.exp(lg_b * jnp.maximum(-diff, 0.0)))
    pos = lax.broadcasted_iota(jnp.int32, (CHUNK, 1), 0).astype(_f32)
    chunk_len = jnp.full((1, 1), float(CHUNK), _f32)
    q_decay_f = jnp.exp(lg_f * (pos + 1.0))
    k_decay_f = jnp.exp(lg_f * (CHUNK - 1.0 - pos))
    gamma_f = jnp.exp(lg_f * chunk_len)
    q_decay_b = jnp.exp(lg_b * (CHUNK - pos))
    k_decay_b = jnp.exp(lg_b * pos)
    gamma_b = jnp.exp(lg_b * chunk_len)

    def scaled(t, d):
        return (t.astype(_f32) * d).astype(_bf16)

    def kv_outer(kc, vc, d):
        return lax.dot_general(scaled(kc, d), vc, (((0,), (0,)), ((), ())),
                               preferred_element_type=_f32)

    def chunk_rows(c):
        return pl.ds(pl.multiple_of(c * CHUNK, CHUNK), CHUNK)

    s_ref[...] = jnp.zeros_like(s_ref)

    def fwd(c, carry):
        rows = chunk_rows(c)
        snap_ref[c] = s_ref[...].astype(_bf16)
        s_ref[...] = gamma_f * s_ref[...] + kv_outer(k_ref[rows, :], v_ref[rows, :], k_decay_f)
        return carry

    lax.fori_loop(0, n_chunks, fwd, 0, unroll=unroll)

    s_ref[...] = jnp.zeros_like(s_ref)
    w = w_ref[...]

    def bwd(t, carry):
        c = jnp.where(t < n_ctx_chunks, n_ctx_chunks - 1 - t, n_chunks - 1 + n_ctx_chunks - t)
        rows = chunk_rows(c)
        qc, kc, vc = q_ref[rows, :], k_ref[rows, :], v_ref[rows, :]
        s = lax.dot_general(qc, kc, (((1,), (1,)), ((), ())), preferred_element_type=_f32)
        o = jnp.dot((s * decay).astype(_bf16), vc, preferred_element_type=_f32)
        o += q_decay_f * jnp.dot(qc, snap_ref[c], preferred_element_type=_f32)
        o += q_decay_b * jnp.dot(qc, s_ref[...].astype(_bf16), preferred_element_type=_f32)
        s_ref[...] = gamma_b * s_ref[...] + kv_outer(kc, vc, k_decay_b)
        r = o * lax.rsqrt(jnp.mean(o * o, axis=-1, keepdims=True) + EPS) * w
        g = g_ref[rows, :].astype(_f32)
        o_ref[rows, :] = (g * (1.0 / (1.0 + jnp.exp(-g))) * r).astype(o_ref.dtype)
        return carry

    lax.fori_loop(0, n_chunks, bwd, 0, unroll=unroll)


def _retention(y, lg, norm_w, n_ctx, heads, head_dim):
    m = y.shape[0]
    n_chunks = m // CHUNK
    ret_width = heads * head_dim
    blk = lambda part: pl.BlockSpec((m, head_dim), lambda h: (0, part * heads + h))
    return pl.pallas_call(
        functools.partial(_retention_kernel, n_ctx_chunks=n_ctx // CHUNK,
                          unroll=max(u for u in (6, 3, 2, 1) if n_chunks % u == 0)),
        out_shape=jax.ShapeDtypeStruct((m, ret_width), _bf16),
        grid=(heads,),
        in_specs=[pl.BlockSpec(memory_space=pltpu.SMEM),
                  blk(0), blk(1), blk(2), blk(3),
                  pl.BlockSpec((1, head_dim), lambda h: (0, h))],
        out_specs=pl.BlockSpec((m, head_dim), lambda h: (0, h)),
        scratch_shapes=[pltpu.VMEM((n_chunks, head_dim, head_dim), _bf16),
                        pltpu.VMEM((head_dim, head_dim), _f32)],
        compiler_params=_params(("arbitrary",)),
        name="retention",
    )(lg, y, y, y, y, norm_w.reshape(1, ret_width))


def _conv_kernel(bg_ref, cg_ref, xc_ref, w_ref, o_ref, *, n_ctx, rows_per_step):
    m = o_ref.shape[0]
    r = rows_per_step
    w0, w1, w2 = w_ref[0:1, :], w_ref[1:2, :], w_ref[2:3, :]
    ridx = lax.broadcasted_iota(jnp.int32, (r, 1), 0)

    def u_rows(start, size):
        rows = pl.ds(pl.multiple_of(start, 16), size)
        return cg_ref[rows, :].astype(_f32) * xc_ref[rows, :].astype(_f32)

    def step(i, prev_last):
        r0 = i * r
        u = u_rows(r0, r)
        nxt_start = jnp.minimum(r0 + r, m - 16)
        nxt_first = u_rows(nxt_start, 16)[0:1, :]
        g = r0 + ridx
        is_start = (g == 0) | (g == n_ctx)
        is_end = (g == n_ctx - 1) | (g == m - 1)
        up = jnp.where(ridx == 0, prev_last, pltpu.roll(u, 1, axis=0))
        up = jnp.where(is_start, 0.0, up)
        dn = jnp.where(ridx == r - 1, nxt_first, pltpu.roll(u, r - 1, axis=0))
        dn = jnp.where(is_end, 0.0, dn)
        conv = up * w0 + u * w1 + dn * w2
        rows = pl.ds(pl.multiple_of(r0, 16), r)
        o_ref[rows, :] = (bg_ref[rows, :].astype(_f32) * conv).astype(o_ref.dtype)
        return u[r - 1:r, :]

    lax.fori_loop(0, m // r, step, jnp.zeros((1, o_ref.shape[1]), _f32))


def _gated_conv(y, conv_w_t, n_ctx, ret_width, conv_width):
    m = y.shape[0]
    bc = _block(conv_width, 256, 128)
    nb = conv_width // bc
    base = 4 * ret_width // bc
    r = _block(n_ctx, 128, 16)
    blk = lambda part: pl.BlockSpec((m, bc), lambda j: (0, base + part * nb + j))
    return pl.pallas_call(
        functools.partial(_conv_kernel, n_ctx=n_ctx, rows_per_step=r),
        out_shape=jax.ShapeDtypeStruct((m, conv_width), _bf16),
        grid=(nb,),
        in_specs=[blk(0), blk(1), blk(2), pl.BlockSpec((3, bc), lambda j: (0, j))],
        out_specs=pl.BlockSpec((m, bc), lambda j: (0, j)),
        compiler_params=_params(("arbitrary",)),
        name="gated_conv",
    )(y, y, y, conv_w_t)


def _out_proj_kernel(r_ref, c_ref, w_hbm, h_ref, g_ref, o_ref, stage_ref, wb_ref, sem, *,
                     layer, n_ctx):
    rows, kr = r_ref.shape
    bn = wb_ref.shape[1]

    @pl.when(pl.program_id(1) == 0)
    def _():
        _fetch_weight_block(w_hbm, stage_ref, wb_ref, sem, layer, pl.program_id(0),
                            pl.num_programs(0), lambda t: (0, t * bn))

    y = (jnp.dot(r_ref[...], wb_ref[0:kr, :], preferred_element_type=_f32)
         + jnp.dot(c_ref[...], wb_ref[kr:, :], preferred_element_type=_f32))
    gate = _row_select(g_ref, pl.program_id(1) * rows, rows, n_ctx)
    o_ref[...] = h_ref[...] + gate * y


def _out_proj(ret, conv, w, layer, h, gate, n_ctx):
    m, kr = ret.shape
    kc = conv.shape[1]
    n = w.shape[2]
    bm = _block(m, 1056, 16)
    bn = _block(n, 512, 128)
    return pl.pallas_call(
        functools.partial(_out_proj_kernel, layer=layer, n_ctx=n_ctx),
        out_shape=jax.ShapeDtypeStruct((m, n), _f32),
        grid=(n // bn, m // bm),
        in_specs=[pl.BlockSpec((bm, kr), lambda j, i: (i, 0)),
                  pl.BlockSpec((bm, kc), lambda j, i: (i, 0)),
                  pl.BlockSpec(memory_space=pl.ANY),
                  pl.BlockSpec((bm, bn), lambda j, i: (i, j)),
                  pl.BlockSpec((ADA_ROWS, bn), lambda j, i: (0, j))],
        out_specs=pl.BlockSpec((bm, bn), lambda j, i: (i, j)),
        scratch_shapes=_weight_scratch(kr + kc, bn),
        compiler_params=_params(("arbitrary", "arbitrary")),
        name="out_proj",
    )(ret, conv, w, h, gate)


def _mlp_up_kernel(a_ref, w_hbm, o_ref, stage_ref, wb_ref, sem, *, layer):
    bn = wb_ref.shape[1]

    @pl.when(pl.program_id(1) == 0)
    def _():
        _fetch_weight_block(w_hbm, stage_ref, wb_ref, sem, layer, pl.program_id(0),
                            pl.num_programs(0), lambda t: (0, t * bn))

    z = jnp.maximum(jnp.dot(a_ref[...], wb_ref[...], preferred_element_type=_f32), 0.0)
    o_ref[...] = (z * z).astype(o_ref.dtype)


def _mlp_up(a, w, layer):
    m, k = a.shape
    n = w.shape[2]
    bm = _block(m, 1056, 16)
    bn = _block(n, 1024, 128)
    return pl.pallas_call(
        functools.partial(_mlp_up_kernel, layer=layer),
        out_shape=jax.ShapeDtypeStruct((m, n), _bf16),
        grid=(n // bn, m // bm),
        in_specs=[pl.BlockSpec((bm, k), lambda j, i: (i, 0)),
                  pl.BlockSpec(memory_space=pl.ANY)],
        out_specs=pl.BlockSpec((bm, bn), lambda j, i: (i, j)),
        scratch_shapes=_weight_scratch(k, bn),
        compiler_params=_params(("arbitrary", "arbitrary")),
        name="mlp_up",
    )(a, w)


def _mlp_down_kernel(z_ref, w_hbm, h_ref, g_ref, o_ref, stage_ref, wb_ref, sem, acc_ref, *,
                     layer, n_ctx):
    j, kk, i = pl.program_id(0), pl.program_id(1), pl.program_id(2)
    nk = pl.num_programs(1)
    bm = z_ref.shape[0]
    bk, bn = wb_ref.shape

    @pl.when(i == 0)
    def _():
        _fetch_weight_block(w_hbm, stage_ref, wb_ref, sem, layer, j * nk + kk,
                            pl.num_programs(0) * nk, lambda t: ((t % nk) * bk, (t // nk) * bn))

    rows = pl.ds(pl.multiple_of(i * bm, bm), bm)

    @pl.when(kk == 0)
    def _():
        acc_ref[rows, :] = jnp.zeros((bm, bn), _f32)

    acc_ref[rows, :] += jnp.dot(z_ref[...], wb_ref[...], preferred_element_type=_f32)

    @pl.when(kk == nk - 1)
    def _():
        gate = _row_select(g_ref, i * bm, bm, n_ctx)
        o_ref[...] = h_ref[...] + gate * acc_ref[rows, :]


def _mlp_down(z, w, layer, h, gate, n_ctx):
    m, k = z.shape
    n = w.shape[2]
    bm = _block(m, 1056, 16)
    bn = _block(n, 512, 128)
    bk = _block(k, 4096, CAST_ROWS)
    nk = k // bk
    res_map = lambda j, kk, i: (jnp.where(kk == nk - 1, i, 0), j)
    return pl.pallas_call(
        functools.partial(_mlp_down_kernel, layer=layer, n_ctx=n_ctx),
        out_shape=jax.ShapeDtypeStruct((m, n), _f32),
        grid=(n // bn, nk, m // bm),
        in_specs=[pl.BlockSpec((bm, bk), lambda j, kk, i: (i, kk)),
                  pl.BlockSpec(memory_space=pl.ANY),
                  pl.BlockSpec((bm, bn), res_map),
                  pl.BlockSpec((ADA_ROWS, bn), lambda j, kk, i: (0, j))],
        out_specs=pl.BlockSpec((bm, bn), res_map),
        scratch_shapes=_weight_scratch(bk, bn) + [pltpu.VMEM((m, bn), _f32)],
        compiler_params=_params(("arbitrary", "arbitrary", "arbitrary")),
        name="mlp_down",
    )(z, w, h, gate)


def _rope_tables(n_lat, n_ctx, head_dim):
    rows = n_lat // GRID_W
    row = jnp.broadcast_to(jnp.arange(rows, dtype=_f32)[:, None], (rows, GRID_W)).reshape(-1)
    col = jnp.broadcast_to(jnp.arange(GRID_W, dtype=_f32)[None, :], (rows, GRID_W)).reshape(-1)
    nf = head_dim // 4
    inv = ROPE_THETA ** (-jnp.arange(nf, dtype=_f32) / nf)
    ang = jnp.concatenate([row[:, None] * inv, col[:, None] * inv], axis=-1)
    cos = jnp.concatenate([jnp.ones((n_ctx, 2 * nf), _f32), jnp.cos(ang)], axis=0)
    sin = jnp.concatenate([jnp.zeros((n_ctx, 2 * nf), _f32), jnp.sin(ang)], axis=0)
    return cos, sin


def kernel(x, c, ctx, c_ctx, ada_w_down, ada_w_up, ada_b, norm1_w, norm2_w, w_in, ret_decay_fwd,
           ret_decay_bwd, ret_norm_w, conv_w, w_out, mlp_w1, mlp_w2, final_norm_w):
    batch, n_lat, d = x.shape
    n_ctx = ctx.shape[1]
    depth = w_in.shape[0]
    heads = ret_decay_fwd.shape[1]
    ret_width = ret_norm_w.shape[1]
    conv_width = conv_w.shape[1]
    head_dim = ret_width // heads
    assert batch == 1 and c.shape[0] == 1
    assert n_ctx % CHUNK == 0 and n_lat % CHUNK == 0 and n_lat % GRID_W == 0
    assert w_in.shape[2] == 4 * ret_width + 3 * conv_width and ret_width + conv_width == d

    s = jnp.zeros((ADA_ROWS, d), _f32).at[0].set(jax.nn.silu(c[0])).at[1].set(jax.nn.silu(c_ctx))
    mods = _ada_mod(s, ada_w_down, ada_w_up, ada_b)
    cos, sin = _rope_tables(n_lat, n_ctx, head_dim)
    lg = jnp.stack([jax.nn.log_sigmoid(ret_decay_fwd.astype(_f32)),
                    jax.nn.log_sigmoid(ret_decay_bwd.astype(_f32))], axis=1)

    h = jnp.concatenate([ctx[0], x[0]], axis=0)
    for l in range(depth):
        sh1, sc1, g1, sh2, sc2, g2 = [mods[l, :, j * d:(j + 1) * d] for j in range(N_MOD)]
        a = _norm_mod(h, norm1_w[l], sh1, sc1, n_ctx)
        y = _in_proj(a, w_in, l, cos, sin, ret_width, head_dim)
        ret = _retention(y, lg[l], ret_norm_w[l], n_ctx, heads, head_dim)
        conv = _gated_conv(y, conv_w[l].T, n_ctx, ret_width, conv_width)
        h = _out_proj(ret, conv, w_out, l, h, g1, n_ctx)
        a = _norm_mod(h, norm2_w[l], sh2, sc2, n_ctx)
        z = _mlp_up(a, mlp_w1, l)
        h = _mlp_down(z, mlp_w2, l, h, g2, n_ctx)
    return _final_norm(h, final_norm_w, n_ctx)[None]
```

```python
import functools

import jax
import jax.numpy as jnp
from jax import lax
from jax.experimental import pallas as pl
from jax.experimental.pallas import tpu as pltpu

GRID_W = 64
CHUNK = 128
ROPE_THETA = 10000.0
EPS = 1e-6
N_MOD = 6
ADA_ROWS = 8
NORM_GROUP = 16
CAST_ROWS = 256
V7X_VMEM_LIMIT_BYTES = 60 * 1024 * 1024

_bf16 = jnp.bfloat16
_f32 = jnp.float32


def _block(total, target, align):
    best = None
    for d in range(align, min(total, target) + 1, align):
        if total % d == 0:
            best = d
    if best is None:
        raise ValueError(f"no block for {total} (target {target}, align {align})")
    return best


def _params(sem):
    return pltpu.CompilerParams(dimension_semantics=sem, vmem_limit_bytes=V7X_VMEM_LIMIT_BYTES)


def _fetch_weight_block(w_hbm, stage_ref, wb_ref, sem, layer, t, n_blocks, origin):
    rows, cols = stage_ref.shape

    def aligned(v, a):
        return v if isinstance(v, int) else pl.multiple_of(v, a)

    def copy(tt):
        r0, c0 = origin(tt)
        src = w_hbm.at[layer, pl.ds(aligned(r0, rows), rows), pl.ds(aligned(c0, cols), cols)]
        return pltpu.make_async_copy(src, stage_ref, sem.at[0])

    @pl.when(t == 0)
    def _():
        copy(t).start()

    copy(t).wait()

    def cast(r, carry):
        sl = pl.ds(pl.multiple_of(r * CAST_ROWS, CAST_ROWS), CAST_ROWS)
        wb_ref[sl, :] = stage_ref[sl, :].astype(_bf16)
        return carry

    lax.fori_loop(0, rows // CAST_ROWS, cast, 0)

    @pl.when(t + 1 < n_blocks)
    def _():
        copy(t + 1).start()


def _weight_scratch(bk, bn):
    assert bk % CAST_ROWS == 0
    return [pltpu.VMEM((bk, bn), _f32), pltpu.VMEM((bk, bn), _bf16), pltpu.SemaphoreType.DMA((1,))]


def _ada_down_kernel(s_ref, wd_ref, t_ref):
    t_ref[0] = jnp.dot(s_ref[...].astype(_bf16), wd_ref[0].astype(_bf16),
                       preferred_element_type=_f32)


def _ada_up_kernel(t_ref, wu_ref, b_ref, m_ref):
    m_ref[0] = jnp.dot(t_ref[0].astype(_bf16), wu_ref[0].astype(_bf16),
                       preferred_element_type=_f32) + b_ref[0]


def _ada_mod(s, w_down, w_up, b):
    depth, d, rank = w_down.shape
    n = w_up.shape[2]
    t = pl.pallas_call(
        _ada_down_kernel,
        out_shape=jax.ShapeDtypeStruct((depth, ADA_ROWS, rank), _f32),
        grid=(depth,),
        in_specs=[pl.BlockSpec((ADA_ROWS, d), lambda l: (0, 0)),
                  pl.BlockSpec((1, d, rank), lambda l: (l, 0, 0))],
        out_specs=pl.BlockSpec((1, ADA_ROWS, rank), lambda l: (l, 0, 0)),
        compiler_params=_params(("arbitrary",)),
        name="ada_down",
    )(s, w_down)
    bn = _block(n, 4096, 128)
    return pl.pallas_call(
        _ada_up_kernel,
        out_shape=jax.ShapeDtypeStruct((depth, ADA_ROWS, n), _f32),
        grid=(depth, n // bn),
        in_specs=[pl.BlockSpec((1, ADA_ROWS, rank), lambda l, j: (l, 0, 0)),
                  pl.BlockSpec((1, rank, bn), lambda l, j: (l, 0, j)),
                  pl.BlockSpec((1, 1, bn), lambda l, j: (l, 0, j))],
        out_specs=pl.BlockSpec((1, ADA_ROWS, bn), lambda l, j: (l, 0, j)),
        compiler_params=_params(("arbitrary", "arbitrary")),
        name="ada_up",
    )(t, w_up, b.reshape(depth, 1, n))


def _row_select(mod_ref, row0, rows, n_ctx):
    ridx = row0 + lax.broadcasted_iota(jnp.int32, (rows, 1), 0)
    return jnp.where(ridx < n_ctx, mod_ref[1:2, :], mod_ref[0:1, :])


def _norm_mod_kernel(x_ref, w_ref, sh_ref, sc_ref, o_ref, gain_ref, shift_ref, rstd_ref, *, n_ctx):
    rows, d = x_ref.shape
    g_rows = NORM_GROUP
    w = w_ref[...]
    for r in range(2):
        gain_ref[r * g_rows:(r + 1) * g_rows, :] = jnp.broadcast_to(
            w * (1.0 + sc_ref[r:r + 1, :]), (g_rows, d))
        shift_ref[r * g_rows:(r + 1) * g_rows, :] = jnp.broadcast_to(sh_ref[r:r + 1, :], (g_rows, d))

    def group(g):
        return pl.ds(pl.multiple_of(g * g_rows, g_rows), g_rows)

    def stats(g, carry):
        x = x_ref[group(g), :]
        rstd_ref[group(g), :] = lax.rsqrt(jnp.mean(x * x, axis=-1, keepdims=True) + EPS)
        return carry

    lax.fori_loop(0, rows // g_rows, stats, 0, unroll=4)

    ctx_groups = jnp.clip(n_ctx - pl.program_id(0) * rows, 0, rows) // g_rows

    def apply(g, carry):
        mod = pl.ds(pl.multiple_of(jnp.where(g < ctx_groups, g_rows, 0), g_rows), g_rows)
        x = x_ref[group(g), :]
        o_ref[group(g), :] = ((x * rstd_ref[group(g), :]) * gain_ref[mod, :]
                              + shift_ref[mod, :]).astype(o_ref.dtype)
        return carry

    lax.fori_loop(0, rows // g_rows, apply, 0, unroll=4)


def _norm_mod(h, w, sh, sc, n_ctx):
    m, d = h.shape
    assert n_ctx % NORM_GROUP == 0
    br = _block(m, 512, 4 * NORM_GROUP)
    return pl.pallas_call(
        functools.partial(_norm_mod_kernel, n_ctx=n_ctx),
        out_shape=jax.ShapeDtypeStruct((m, d), _bf16),
        grid=(m // br,),
        in_specs=[pl.BlockSpec((br, d), lambda i: (i, 0)),
                  pl.BlockSpec((1, d), lambda i: (0, 0)),
                  pl.BlockSpec((ADA_ROWS, d), lambda i: (0, 0)),
                  pl.BlockSpec((ADA_ROWS, d), lambda i: (0, 0))],
        out_specs=pl.BlockSpec((br, d), lambda i: (i, 0)),
        scratch_shapes=[pltpu.VMEM((2 * NORM_GROUP, d), _f32),
                        pltpu.VMEM((2 * NORM_GROUP, d), _f32),
                        pltpu.VMEM((br, 1), _f32)],
        compiler_params=_params(("arbitrary",)),
        name="norm_mod",
    )(h, w.reshape(1, d), sh, sc)


def _final_norm_kernel(x_ref, w_ref, o_ref):
    x = x_ref[...]
    o_ref[...] = x * lax.rsqrt(jnp.mean(x * x, axis=-1, keepdims=True) + EPS) * w_ref[...]


def _final_norm(h, w, n_ctx):
    m, d = h.shape
    br = _block(n_ctx, 512, 8)
    skip = n_ctx // br
    return pl.pallas_call(
        _final_norm_kernel,
        out_shape=jax.ShapeDtypeStruct((m - n_ctx, d), _f32),
        grid=((m - n_ctx) // br,),
        in_specs=[pl.BlockSpec((br, d), lambda i: (i + skip, 0)),
                  pl.BlockSpec((1, d), lambda i: (0, 0))],
        out_specs=pl.BlockSpec((br, d), lambda i: (i, 0)),
        compiler_params=_params(("arbitrary",)),
        name="final_norm",
    )(h, w.reshape(1, d))


def _in_proj_rope_kernel(a_ref, w_hbm, cos_ref, sin_ref, o_ref, stage_ref, wb_ref, sem, *,
                         layer, n_q_blocks, head_dim):
    j = pl.program_id(0)
    bn = wb_ref.shape[1]

    @pl.when(pl.program_id(1) == 0)
    def _():
        _fetch_weight_block(w_hbm, stage_ref, wb_ref, sem, layer, j, pl.num_programs(0),
                            lambda t: (0, t * bn))

    y = jnp.dot(a_ref[...], wb_ref[...], preferred_element_type=_f32)
    half = head_dim // 2
    scale = jnp.where(j < n_q_blocks, head_dim ** -0.5, 1.0).astype(_f32)
    cos = cos_ref[...] * scale
    sin = sin_ref[...] * scale
    for hd in range(bn // head_dim):
        c0 = hd * head_dim
        t1 = y[:, c0:c0 + half]
        t2 = y[:, c0 + half:c0 + head_dim]
        o_ref[:, c0:c0 + half] = (t1 * cos - t2 * sin).astype(o_ref.dtype)
        o_ref[:, c0 + half:c0 + head_dim] = (t1 * sin + t2 * cos).astype(o_ref.dtype)


def _in_proj_plain_kernel(a_ref, w_hbm, o_ref, stage_ref, wb_ref, sem, *, layer, col0):
    bn = wb_ref.shape[1]

    @pl.when(pl.program_id(1) == 0)
    def _():
        _fetch_weight_block(w_hbm, stage_ref, wb_ref, sem, layer, pl.program_id(0),
                            pl.num_programs(0), lambda t: (0, col0 + t * bn))

    o_ref[...] = jnp.dot(a_ref[...], wb_ref[...], preferred_element_type=_f32).astype(o_ref.dtype)


def _in_proj(a, w, layer, cos, sin, ret_width, head_dim):
    m, k = a.shape
    n = w.shape[2]
    bm = _block(m, 1056, 16)
    bn = _block(ret_width, 1024, head_dim)
    n_rope = 2 * ret_width
    assert (n - n_rope) % bn == 0
    half = head_dim // 2
    a_spec = pl.BlockSpec((bm, k), lambda j, i: (i, 0))
    o_spec = pl.BlockSpec((bm, bn), lambda j, i: (i, j))
    qk = pl.pallas_call(
        functools.partial(_in_proj_rope_kernel, layer=layer, n_q_blocks=ret_width // bn,
                          head_dim=head_dim),
        out_shape=jax.ShapeDtypeStruct((m, n_rope), _bf16),
        grid=(n_rope // bn, m // bm),
        in_specs=[a_spec, pl.BlockSpec(memory_space=pl.ANY),
                  pl.BlockSpec((bm, half), lambda j, i: (i, 0)),
                  pl.BlockSpec((bm, half), lambda j, i: (i, 0))],
        out_specs=o_spec,
        scratch_shapes=_weight_scratch(k, bn),
        compiler_params=_params(("arbitrary", "arbitrary")),
        name="in_proj_qk",
    )(a, w, cos, sin)
    rest = pl.pallas_call(
        functools.partial(_in_proj_plain_kernel, layer=layer, col0=n_rope),
        out_shape=jax.ShapeDtypeStruct((m, n - n_rope), _bf16),
        grid=((n - n_rope) // bn, m // bm),
        in_specs=[a_spec, pl.BlockSpec(memory_space=pl.ANY)],
        out_specs=o_spec,
        scratch_shapes=_weight_scratch(k, bn),
        compiler_params=_params(("arbitrary", "arbitrary")),
        name="in_proj_rest",
    )(a, w)
    return qk, rest


def _retention_kernel(lg_ref, q_ref, k_ref, v_ref, g_ref, w_ref, o_ref, snap_ref, s_ref, *,
                      n_ctx_chunks, unroll):
    head = pl.program_id(0)
    n_chunks = q_ref.shape[0] // CHUNK
    lg_f = lg_ref[0, head]
    lg_b = lg_ref[1, head]

    ii = lax.broadcasted_iota(jnp.int32, (CHUNK, CHUNK), 0)
    jj = lax.broadcasted_iota(jnp.int32, (CHUNK, CHUNK), 1)
    diff = (ii - jj).astype(_f32)
    decay = jnp.where(diff >= 0.0, jnp.exp(lg_f * jnp.maximum(diff, 0.0)),
                      jnp.exp(lg_b * jnp.maximum(-diff, 0.0)))
    pos = lax.broadcasted_iota(jnp.int32, (CHUNK, 1), 0).astype(_f32)
    chunk_len = jnp.full((1, 1), float(CHUNK), _f32)
    q_decay_f = jnp.exp(lg_f * (pos + 1.0))
    k_decay_f = jnp.exp(lg_f * (CHUNK - 1.0 - pos))
    gamma_f = jnp.exp(lg_f * chunk_len)
    q_decay_b = jnp.exp(lg_b * (CHUNK - pos))
    k_decay_b = jnp.exp(lg_b * pos)
    gamma_b = jnp.exp(lg_b * chunk_len)

    def scaled(t, d):
        return (t.astype(_f32) * d).astype(_bf16)

    def kv_outer(kc, vc, d):
        return lax.dot_general(scaled(kc, d), vc, (((0,), (0,)), ((), ())),
                               preferred_element_type=_f32)

    def chunk_rows(c):
        return pl.ds(pl.multiple_of(c * CHUNK, CHUNK), CHUNK)

    s_ref[...] = jnp.zeros_like(s_ref)

    def fwd(c, carry):
        rows = chunk_rows(c)
        snap_ref[c] = s_ref[...].astype(_bf16)
        s_ref[...] = gamma_f * s_ref[...] + kv_outer(k_ref[rows, :], v_ref[rows, :], k_decay_f)
        return carry

    lax.fori_loop(0, n_chunks, fwd, 0, unroll=unroll)

    s_ref[...] = jnp.zeros_like(s_ref)
    w = w_ref[...]

    def bwd(t, carry):
        c = jnp.where(t < n_ctx_chunks, n_ctx_chunks - 1 - t, n_chunks - 1 + n_ctx_chunks - t)
        rows = chunk_rows(c)
        qc, kc, vc = q_ref[rows, :], k_ref[rows, :], v_ref[rows, :]
        s = lax.dot_general(qc, kc, (((1,), (1,)), ((), ())), preferred_element_type=_f32)
        o = jnp.dot((s * decay).astype(_bf16), vc, preferred_element_type=_f32)
        o += q_decay_f * jnp.dot(qc, snap_ref[c], preferred_element_type=_f32)
        o += q_decay_b * jnp.dot(qc, s_ref[...].astype(_bf16), preferred_element_type=_f32)
        s_ref[...] = gamma_b * s_ref[...] + kv_outer(kc, vc, k_decay_b)
        r = o * lax.rsqrt(jnp.mean(o * o, axis=-1, keepdims=True) + EPS) * w
        g = g_ref[rows, :].astype(_f32)
        o_ref[rows, :] = (g * (1.0 / (1.0 + jnp.exp(-g))) * r).astype(o_ref.dtype)
        return carry

    lax.fori_loop(0, n_chunks, bwd, 0, unroll=unroll)


def _retention(qk, rest, lg, norm_w, n_ctx, heads, head_dim):
    m = qk.shape[0]
    n_chunks = m // CHUNK
    ret_width = heads * head_dim
    blk = lambda part: pl.BlockSpec((m, head_dim), lambda h: (0, part * heads + h))
    return pl.pallas_call(
        functools.partial(_retention_kernel, n_ctx_chunks=n_ctx // CHUNK,
                          unroll=max(u for u in (6, 3, 2, 1) if n_chunks % u == 0)),
        out_shape=jax.ShapeDtypeStruct((m, ret_width), _bf16),
        grid=(heads,),
        in_specs=[pl.BlockSpec(memory_space=pltpu.SMEM),
                  blk(0), blk(1), blk(0), blk(1),
                  pl.BlockSpec((1, head_dim), lambda h: (0, h))],
        out_specs=pl.BlockSpec((m, head_dim), lambda h: (0, h)),
        scratch_shapes=[pltpu.VMEM((n_chunks, head_dim, head_dim), _bf16),
                        pltpu.VMEM((head_dim, head_dim), _f32)],
        compiler_params=_params(("arbitrary",)),
        name="retention",
    )(lg, qk, qk, rest, rest, norm_w.reshape(1, ret_width))


def _conv_kernel(bg_ref, cg_ref, xc_ref, w_ref, o_ref, *, n_ctx, rows_per_step):
    m = o_ref.shape[0]
    r = rows_per_step
    w0, w1, w2 = w_ref[0:1, :], w_ref[1:2, :], w_ref[2:3, :]
    ridx = lax.broadcasted_iota(jnp.int32, (r, 1), 0)

    def u_rows(start, size):
        rows = pl.ds(pl.multiple_of(start, 16), size)
        return cg_ref[rows, :].astype(_f32) * xc_ref[rows, :].astype(_f32)

    def step(i, prev_last):
        r0 = i * r
        u = u_rows(r0, r)
        nxt_start = jnp.minimum(r0 + r, m - 16)
        nxt_first = u_rows(nxt_start, 16)[0:1, :]
        g = r0 + ridx
        is_start = (g == 0) | (g == n_ctx)
        is_end = (g == n_ctx - 1) | (g == m - 1)
        up = jnp.where(ridx == 0, prev_last, pltpu.roll(u, 1, axis=0))
        up = jnp.where(is_start, 0.0, up)
        dn = jnp.where(ridx == r - 1, nxt_first, pltpu.roll(u, r - 1, axis=0))
        dn = jnp.where(is_end, 0.0, dn)
        conv = up * w0 + u * w1 + dn * w2
        rows = pl.ds(pl.multiple_of(r0, 16), r)
        o_ref[rows, :] = (bg_ref[rows, :].astype(_f32) * conv).astype(o_ref.dtype)
        return u[r - 1:r, :]

    lax.fori_loop(0, m // r, step, jnp.zeros((1, o_ref.shape[1]), _f32))


def _gated_conv(y, conv_w_t, n_ctx, ret_width, conv_width):
    m = y.shape[0]
    bc = _block(conv_width, 256, 128)
    nb = conv_width // bc
    base = 2 * ret_width // bc
    r = _block(n_ctx, 128, 16)
    blk = lambda part: pl.BlockSpec((m, bc), lambda j: (0, base + part * nb + j))
    return pl.pallas_call(
        functools.partial(_conv_kernel, n_ctx=n_ctx, rows_per_step=r),
        out_shape=jax.ShapeDtypeStruct((m, conv_width), _bf16),
        grid=(nb,),
        in_specs=[blk(0), blk(1), blk(2), pl.BlockSpec((3, bc), lambda j: (0, j))],
        out_specs=pl.BlockSpec((m, bc), lambda j: (0, j)),
        compiler_params=_params(("arbitrary",)),
        name="gated_conv",
    )(y, y, y, conv_w_t)


def _out_proj_kernel(r_ref, c_ref, w_hbm, h_ref, g_ref, o_ref, stage_ref, wb_ref, sem, *,
                     layer, n_ctx):
    rows, kr = r_ref.shape
    bn = wb_ref.shape[1]

    @pl.when(pl.program_id(1) == 0)
    def _():
        _fetch_weight_block(w_hbm, stage_ref, wb_ref, sem, layer, pl.program_id(0),
                            pl.num_programs(0), lambda t: (0, t * bn))

    y = (jnp.dot(r_ref[...], wb_ref[0:kr, :], preferred_element_type=_f32)
         + jnp.dot(c_ref[...], wb_ref[kr:, :], preferred_element_type=_f32))
    gate = _row_select(g_ref, pl.program_id(1) * rows, rows, n_ctx)
    o_ref[...] = h_ref[...] + gate * y


def _out_proj(ret, conv, w, layer, h, gate, n_ctx):
    m, kr = ret.shape
    kc = conv.shape[1]
    n = w.shape[2]
    bm = _block(m, 528, 16)
    bn = _block(n, 1024, 128)
    return pl.pallas_call(
        functools.partial(_out_proj_kernel, layer=layer, n_ctx=n_ctx),
        out_shape=jax.ShapeDtypeStruct((m, n), _f32),
        grid=(n // bn, m // bm),
        in_specs=[pl.BlockSpec((bm, kr), lambda j, i: (i, 0)),
                  pl.BlockSpec((bm, kc), lambda j, i: (i, 0)),
                  pl.BlockSpec(memory_space=pl.ANY),
                  pl.BlockSpec((bm, bn), lambda j, i: (i, j)),
                  pl.BlockSpec((ADA_ROWS, bn), lambda j, i: (0, j))],
        out_specs=pl.BlockSpec((bm, bn), lambda j, i: (i, j)),
        scratch_shapes=_weight_scratch(kr + kc, bn),
        compiler_params=_params(("arbitrary", "arbitrary")),
        name="out_proj",
    )(ret, conv, w, h, gate)


def _mlp_up_kernel(a_ref, w_hbm, o_ref, stage_ref, wb_ref, sem, *, layer):
    bn = wb_ref.shape[1]

    @pl.when(pl.program_id(1) == 0)
    def _():
        _fetch_weight_block(w_hbm, stage_ref, wb_ref, sem, layer, pl.program_id(0),
                            pl.num_programs(0), lambda t: (0, t * bn))

    z = jnp.maximum(jnp.dot(a_ref[...], wb_ref[...], preferred_element_type=_f32), 0.0)
    o_ref[...] = (z * z).astype(o_ref.dtype)


def _mlp_up(a, w, layer):
    m, k = a.shape
    n = w.shape[2]
    bm = _block(m, 1056, 16)
    bn = _block(n, 1024, 128)
    return pl.pallas_call(
        functools.partial(_mlp_up_kernel, layer=layer),
        out_shape=jax.ShapeDtypeStruct((m, n), _bf16),
        grid=(n // bn, m // bm),
        in_specs=[pl.BlockSpec((bm, k), lambda j, i: (i, 0)),
                  pl.BlockSpec(memory_space=pl.ANY)],
        out_specs=pl.BlockSpec((bm, bn), lambda j, i: (i, j)),
        scratch_shapes=_weight_scratch(k, bn),
        compiler_params=_params(("arbitrary", "arbitrary")),
        name="mlp_up",
    )(a, w)


def _mlp_down_kernel(z_ref, w_hbm, h_ref, g_ref, o_ref, stage_ref, wb_ref, sem, acc_ref, *,
                     layer, n_ctx, nk):
    j, kk, i = pl.program_id(0), pl.program_id(1), pl.program_id(2)
    bm = z_ref.shape[0]
    bk, bn = wb_ref.shape

    @pl.when(i == 0)
    def _():
        _fetch_weight_block(w_hbm, stage_ref, wb_ref, sem, layer, j * nk + kk,
                            pl.num_programs(0) * nk, lambda t: ((t % nk) * bk, (t // nk) * bn))

    rows = pl.ds(pl.multiple_of(i * bm, bm), bm)

    def partial_product():
        return jnp.dot(z_ref[...], wb_ref[...], preferred_element_type=_f32)

    def finish(total):
        gate = _row_select(g_ref, i * bm, bm, n_ctx)
        o_ref[...] = h_ref[...] + gate * total

    if nk == 1:
        finish(partial_product())
        return

    @pl.when(kk == 0)
    def _():
        acc_ref[rows, :] = partial_product()

    if nk > 2:
        @pl.when((kk > 0) & (kk < nk - 1))
        def _():
            acc_ref[rows, :] += partial_product()

    @pl.when(kk == nk - 1)
    def _():
        finish(acc_ref[rows, :] + partial_product())


def _mlp_down(z, w, layer, h, gate, n_ctx):
    m, k = z.shape
    n = w.shape[2]
    bm = _block(m, 1056, 16)
    bn = _block(n, 512, 128)
    bk = _block(k, 4096, CAST_ROWS)
    nk = k // bk
    res_map = lambda j, kk, i: (jnp.where(kk == nk - 1, i, 0), j)
    return pl.pallas_call(
        functools.partial(_mlp_down_kernel, layer=layer, n_ctx=n_ctx, nk=nk),
        out_shape=jax.ShapeDtypeStruct((m, n), _f32),
        grid=(n // bn, nk, m // bm),
        in_specs=[pl.BlockSpec((bm, bk), lambda j, kk, i: (i, kk)),
                  pl.BlockSpec(memory_space=pl.ANY),
                  pl.BlockSpec((bm, bn), res_map),
                  pl.BlockSpec((ADA_ROWS, bn), lambda j, kk, i: (0, j))],
        out_specs=pl.BlockSpec((bm, bn), res_map),
        scratch_shapes=_weight_scratch(bk, bn) + [pltpu.VMEM((m, bn), _f32)],
        compiler_params=_params(("arbitrary", "arbitrary", "arbitrary")),
        name="mlp_down",
    )(z, w, h, gate)


def _rope_tables(n_lat, n_ctx, head_dim):
    rows = n_lat // GRID_W
    row = jnp.broadcast_to(jnp.arange(rows, dtype=_f32)[:, None], (rows, GRID_W)).reshape(-1)
    col = jnp.broadcast_to(jnp.arange(GRID_W, dtype=_f32)[None, :], (rows, GRID_W)).reshape(-1)
    nf = head_dim // 4
    inv = ROPE_THETA ** (-jnp.arange(nf, dtype=_f32) / nf)
    ang = jnp.concatenate([row[:, None] * inv, col[:, None] * inv], axis=-1)
    cos = jnp.concatenate([jnp.ones((n_ctx, 2 * nf), _f32), jnp.cos(ang)], axis=0)
    sin = jnp.concatenate([jnp.zeros((n_ctx, 2 * nf), _f32), jnp.sin(ang)], axis=0)
    return cos, sin


def kernel(x, c, ctx, c_ctx, ada_w_down, ada_w_up, ada_b, norm1_w, norm2_w, w_in, ret_decay_fwd,
           ret_decay_bwd, ret_norm_w, conv_w, w_out, mlp_w1, mlp_w2, final_norm_w):
    batch, n_lat, d = x.shape
    n_ctx = ctx.shape[1]
    depth = w_in.shape[0]
    heads = ret_decay_fwd.shape[1]
    ret_width = ret_norm_w.shape[1]
    conv_width = conv_w.shape[1]
    head_dim = ret_width // heads
    assert batch == 1 and c.shape[0] == 1
    assert n_ctx % CHUNK == 0 and n_lat % CHUNK == 0 and n_lat % GRID_W == 0
    assert w_in.shape[2] == 4 * ret_width + 3 * conv_width and ret_width + conv_width == d

    s = jnp.zeros((ADA_ROWS, d), _f32).at[0].set(jax.nn.silu(c[0])).at[1].set(jax.nn.silu(c_ctx))
    mods = _ada_mod(s, ada_w_down, ada_w_up, ada_b)
    cos, sin = _rope_tables(n_lat, n_ctx, head_dim)
    lg = jnp.stack([jax.nn.log_sigmoid(ret_decay_fwd.astype(_f32)),
                    jax.nn.log_sigmoid(ret_decay_bwd.astype(_f32))], axis=1)

    h = jnp.concatenate([ctx[0], x[0]], axis=0)
    for l in range(depth):
        sh1, sc1, g1, sh2, sc2, g2 = [mods[l, :, j * d:(j + 1) * d] for j in range(N_MOD)]
        a = _norm_mod(h, norm1_w[l], sh1, sc1, n_ctx)
        qk, rest = _in_proj(a, w_in, l, cos, sin, ret_width, head_dim)
        ret = _retention(qk, rest, lg[l], ret_norm_w[l], n_ctx, heads, head_dim)
        conv = _gated_conv(rest, conv_w[l].T, n_ctx, ret_width, conv_width)
        h = _out_proj(ret, conv, w_out, l, h, g1, n_ctx)
        a = _norm_mod(h, norm2_w[l], sh2, sc2, n_ctx)
        z = _mlp_up(a, mlp_w1, l)
        h = _mlp_down(z, mlp_w2, l, h, g2, n_ctx)
    return _final_norm(h, final_norm_w, n_ctx)[None]
```

```python
import functools

import jax
import jax.numpy as jnp
from jax import lax
from jax.experimental import pallas as pl
from jax.experimental.pallas import tpu as pltpu

GRID_W = 64
CHUNK = 128
ROPE_THETA = 10000.0
EPS = 1e-6
N_MOD = 6
ADA_ROWS = 8
NORM_GROUP = 16
CAST_ROWS = 256
RET_CHUNKS = (256, 128)
V7X_VMEM_LIMIT_BYTES = 60 * 1024 * 1024

_bf16 = jnp.bfloat16
_f32 = jnp.float32


def _block(total, target, align):
    best = None
    for d in range(align, min(total, target) + 1, align):
        if total % d == 0:
            best = d
    if best is None:
        raise ValueError(f"no block for {total} (target {target}, align {align})")
    return best


def _params(sem):
    return pltpu.CompilerParams(dimension_semantics=sem, vmem_limit_bytes=V7X_VMEM_LIMIT_BYTES)


def _fetch_weight_block(w_hbm, stage_ref, wb_ref, sem, layer, t, n_blocks, origin):
    rows, cols = stage_ref.shape

    def aligned(v, a):
        return v if isinstance(v, int) else pl.multiple_of(v, a)

    def copy(tt):
        r0, c0 = origin(tt)
        src = w_hbm.at[layer, pl.ds(aligned(r0, rows), rows), pl.ds(aligned(c0, cols), cols)]
        return pltpu.make_async_copy(src, stage_ref, sem.at[0])

    @pl.when(t == 0)
    def _():
        copy(t).start()

    copy(t).wait()

    def cast(r, carry):
        sl = pl.ds(pl.multiple_of(r * CAST_ROWS, CAST_ROWS), CAST_ROWS)
        wb_ref[sl, :] = stage_ref[sl, :].astype(_bf16)
        return carry

    lax.fori_loop(0, rows // CAST_ROWS, cast, 0)

    @pl.when(t + 1 < n_blocks)
    def _():
        copy(t + 1).start()


def _weight_scratch(bk, bn):
    assert bk % CAST_ROWS == 0
    return [pltpu.VMEM((bk, bn), _f32), pltpu.VMEM((bk, bn), _bf16), pltpu.SemaphoreType.DMA((1,))]


def _ada_down_kernel(s_ref, wd_ref, t_ref):
    t_ref[0] = jnp.dot(s_ref[...].astype(_bf16), wd_ref[0].astype(_bf16),
                       preferred_element_type=_f32)


def _ada_up_kernel(t_ref, wu_ref, b_ref, m_ref):
    m_ref[0] = jnp.dot(t_ref[0].astype(_bf16), wu_ref[0].astype(_bf16),
                       preferred_element_type=_f32) + b_ref[0]


def _ada_mod(s, w_down, w_up, b):
    depth, d, rank = w_down.shape
    n = w_up.shape[2]
    t = pl.pallas_call(
        _ada_down_kernel,
        out_shape=jax.ShapeDtypeStruct((depth, ADA_ROWS, rank), _f32),
        grid=(depth,),
        in_specs=[pl.BlockSpec((ADA_ROWS, d), lambda l: (0, 0)),
                  pl.BlockSpec((1, d, rank), lambda l: (l, 0, 0))],
        out_specs=pl.BlockSpec((1, ADA_ROWS, rank), lambda l: (l, 0, 0)),
        compiler_params=_params(("arbitrary",)),
        name="ada_down",
    )(s, w_down)
    bn = _block(n, 4096, 128)
    return pl.pallas_call(
        _ada_up_kernel,
        out_shape=jax.ShapeDtypeStruct((depth, ADA_ROWS, n), _f32),
        grid=(depth, n // bn),
        in_specs=[pl.BlockSpec((1, ADA_ROWS, rank), lambda l, j: (l, 0, 0)),
                  pl.BlockSpec((1, rank, bn), lambda l, j: (l, 0, j)),
                  pl.BlockSpec((1, 1, bn), lambda l, j: (l, 0, j))],
        out_specs=pl.BlockSpec((1, ADA_ROWS, bn), lambda l, j: (l, 0, j)),
        compiler_params=_params(("arbitrary", "arbitrary")),
        name="ada_up",
    )(t, w_up, b.reshape(depth, 1, n))


def _row_select(mod_ref, row0, rows, n_ctx):
    ridx = row0 + lax.broadcasted_iota(jnp.int32, (rows, 1), 0)
    return jnp.where(ridx < n_ctx, mod_ref[1:2, :], mod_ref[0:1, :])


def _norm_mod_kernel(x_ref, w_ref, sh_ref, sc_ref, o_ref, gain_ref, shift_ref, rstd_ref, *, n_ctx):
    rows, d = x_ref.shape
    g_rows = NORM_GROUP
    w = w_ref[...]
    for r in range(2):
        gain_ref[r * g_rows:(r + 1) * g_rows, :] = jnp.broadcast_to(
            w * (1.0 + sc_ref[r:r + 1, :]), (g_rows, d))
        shift_ref[r * g_rows:(r + 1) * g_rows, :] = jnp.broadcast_to(sh_ref[r:r + 1, :], (g_rows, d))

    def group(g):
        return pl.ds(pl.multiple_of(g * g_rows, g_rows), g_rows)

    def stats(g, carry):
        x = x_ref[group(g), :]
        rstd_ref[group(g), :] = lax.rsqrt(jnp.mean(x * x, axis=-1, keepdims=True) + EPS)
        return carry

    lax.fori_loop(0, rows // g_rows, stats, 0, unroll=4)

    ctx_groups = jnp.clip(n_ctx - pl.program_id(0) * rows, 0, rows) // g_rows

    def apply(g, carry):
        mod = pl.ds(pl.multiple_of(jnp.where(g < ctx_groups, g_rows, 0), g_rows), g_rows)
        x = x_ref[group(g), :]
        o_ref[group(g), :] = ((x * rstd_ref[group(g), :]) * gain_ref[mod, :]
                              + shift_ref[mod, :]).astype(o_ref.dtype)
        return carry

    lax.fori_loop(0, rows // g_rows, apply, 0, unroll=4)


def _norm_mod_first_kernel(ctx_ref, x_ref, w_ref, sh_ref, sc_ref, h_ref, o_ref, gain_ref,
                           shift_ref, rstd_ref, *, n_ctx):
    in_ctx = pl.program_id(0) * h_ref.shape[0] < n_ctx

    @pl.when(in_ctx)
    def _():
        h_ref[...] = ctx_ref[...]

    @pl.when(jnp.logical_not(in_ctx))
    def _():
        h_ref[...] = x_ref[...]

    _norm_mod_kernel(h_ref, w_ref, sh_ref, sc_ref, o_ref, gain_ref, shift_ref, rstd_ref,
                     n_ctx=n_ctx)


def _norm_mod_first(ctx, x, w, sh, sc):
    n_ctx, d = ctx.shape
    m = n_ctx + x.shape[0]
    assert n_ctx % (4 * NORM_GROUP) == 0
    br = _block(n_ctx, 512, 4 * NORM_GROUP)
    assert x.shape[0] % br == 0
    ctx_blocks = n_ctx // br
    row_spec = pl.BlockSpec((br, d), lambda i: (i, 0))
    vec_spec = pl.BlockSpec((ADA_ROWS, d), lambda i: (0, 0))
    return pl.pallas_call(
        functools.partial(_norm_mod_first_kernel, n_ctx=n_ctx),
        out_shape=(jax.ShapeDtypeStruct((m, d), _f32), jax.ShapeDtypeStruct((m, d), _bf16)),
        grid=(m // br,),
        in_specs=[pl.BlockSpec((br, d), lambda i: (jnp.minimum(i, ctx_blocks - 1), 0)),
                  pl.BlockSpec((br, d), lambda i: (jnp.maximum(i - ctx_blocks, 0), 0)),
                  pl.BlockSpec((1, d), lambda i: (0, 0)), vec_spec, vec_spec],
        out_specs=(row_spec, row_spec),
        scratch_shapes=[pltpu.VMEM((2 * NORM_GROUP, d), _f32),
                        pltpu.VMEM((2 * NORM_GROUP, d), _f32),
                        pltpu.VMEM((br, 1), _f32)],
        compiler_params=_params(("arbitrary",)),
        name="norm_mod_first",
    )(ctx, x, w.reshape(1, d), sh, sc)


def _norm_mod(h, w, sh, sc, n_ctx):
    m, d = h.shape
    assert n_ctx % NORM_GROUP == 0
    br = _block(m, 512, 4 * NORM_GROUP)
    return pl.pallas_call(
        functools.partial(_norm_mod_kernel, n_ctx=n_ctx),
        out_shape=jax.ShapeDtypeStruct((m, d), _bf16),
        grid=(m // br,),
        in_specs=[pl.BlockSpec((br, d), lambda i: (i, 0)),
                  pl.BlockSpec((1, d), lambda i: (0, 0)),
                  pl.BlockSpec((ADA_ROWS, d), lambda i: (0, 0)),
                  pl.BlockSpec((ADA_ROWS, d), lambda i: (0, 0))],
        out_specs=pl.BlockSpec((br, d), lambda i: (i, 0)),
        scratch_shapes=[pltpu.VMEM((2 * NORM_GROUP, d), _f32),
                        pltpu.VMEM((2 * NORM_GROUP, d), _f32),
                        pltpu.VMEM((br, 1), _f32)],
        compiler_params=_params(("arbitrary",)),
        name="norm_mod",
    )(h, w.reshape(1, d), sh, sc)


def _final_norm_kernel(x_ref, w_ref, o_ref):
    x = x_ref[...]
    o_ref[...] = x * lax.rsqrt(jnp.mean(x * x, axis=-1, keepdims=True) + EPS) * w_ref[...]


def _final_norm(h, w, n_ctx):
    m, d = h.shape
    br = _block(n_ctx, 512, 8)
    skip = n_ctx // br
    return pl.pallas_call(
        _final_norm_kernel,
        out_shape=jax.ShapeDtypeStruct((m - n_ctx, d), _f32),
        grid=((m - n_ctx) // br,),
        in_specs=[pl.BlockSpec((br, d), lambda i: (i + skip, 0)),
                  pl.BlockSpec((1, d), lambda i: (0, 0))],
        out_specs=pl.BlockSpec((br, d), lambda i: (i, 0)),
        compiler_params=_params(("arbitrary",)),
        name="final_norm",
    )(h, w.reshape(1, d))


def _in_proj_rope_kernel(a_ref, w_hbm, cos_ref, sin_ref, o_ref, stage_ref, wb_ref, sem, *,
                         layer, n_q_blocks, head_dim):
    j = pl.program_id(0)
    bn = wb_ref.shape[1]

    @pl.when(pl.program_id(1) == 0)
    def _():
        _fetch_weight_block(w_hbm, stage_ref, wb_ref, sem, layer, j, pl.num_programs(0),
                            lambda t: (0, t * bn))

    y = jnp.dot(a_ref[...], wb_ref[...], preferred_element_type=_f32)
    half = head_dim // 2
    scale = jnp.where(j < n_q_blocks, head_dim ** -0.5, 1.0).astype(_f32)
    cos = cos_ref[...] * scale
    sin = sin_ref[...] * scale
    for hd in range(bn // head_dim):
        c0 = hd * head_dim
        t1 = y[:, c0:c0 + half]
        t2 = y[:, c0 + half:c0 + head_dim]
        o_ref[:, c0:c0 + half] = (t1 * cos - t2 * sin).astype(o_ref.dtype)
        o_ref[:, c0 + half:c0 + head_dim] = (t1 * sin + t2 * cos).astype(o_ref.dtype)


def _in_proj_plain_kernel(a_ref, w_hbm, o_ref, stage_ref, wb_ref, sem, *, layer, col0):
    bn = wb_ref.shape[1]

    @pl.when(pl.program_id(1) == 0)
    def _():
        _fetch_weight_block(w_hbm, stage_ref, wb_ref, sem, layer, pl.program_id(0),
                            pl.num_programs(0), lambda t: (0, col0 + t * bn))

    o_ref[...] = jnp.dot(a_ref[...], wb_ref[...], preferred_element_type=_f32).astype(o_ref.dtype)


def _in_proj(a, w, layer, cos, sin, ret_width, head_dim):
    m, k = a.shape
    n = w.shape[2]
    bm = _block(m, 1056, 16)
    bn = _block(ret_width, 1024, head_dim)
    n_rope = 2 * ret_width
    assert (n - n_rope) % bn == 0
    half = head_dim // 2
    a_spec = pl.BlockSpec((bm, k), lambda j, i: (i, 0))
    o_spec = pl.BlockSpec((bm, bn), lambda j, i: (i, j))
    qk = pl.pallas_call(
        functools.partial(_in_proj_rope_kernel, layer=layer, n_q_blocks=ret_width // bn,
                          head_dim=head_dim),
        out_shape=jax.ShapeDtypeStruct((m, n_rope), _bf16),
        grid=(n_rope // bn, m // bm),
        in_specs=[a_spec, pl.BlockSpec(memory_space=pl.ANY),
                  pl.BlockSpec((bm, half), lambda j, i: (i, 0)),
                  pl.BlockSpec((bm, half), lambda j, i: (i, 0))],
        out_specs=o_spec,
        scratch_shapes=_weight_scratch(k, bn),
        compiler_params=_params(("arbitrary", "arbitrary")),
        name="in_proj_qk",
    )(a, w, cos, sin)
    rest = pl.pallas_call(
        functools.partial(_in_proj_plain_kernel, layer=layer, col0=n_rope),
        out_shape=jax.ShapeDtypeStruct((m, n - n_rope), _bf16),
        grid=((n - n_rope) // bn, m // bm),
        in_specs=[a_spec, pl.BlockSpec(memory_space=pl.ANY)],
        out_specs=o_spec,
        scratch_shapes=_weight_scratch(k, bn),
        compiler_params=_params(("arbitrary", "arbitrary")),
        name="in_proj_rest",
    )(a, w)
    return qk, rest


def _retention_kernel(lg_ref, q_ref, k_ref, v_ref, g_ref, w_ref, o_ref, snap_ref, s_ref,
                      decay_ref, *, chunk, n_ctx_chunks, unroll):
    head = pl.program_id(0)
    n_chunks = q_ref.shape[0] // chunk
    lg_f = lg_ref[0, head]
    lg_b = lg_ref[1, head]

    ii = lax.broadcasted_iota(jnp.int32, (chunk, chunk), 0)
    jj = lax.broadcasted_iota(jnp.int32, (chunk, chunk), 1)
    diff = (ii - jj).astype(_f32)
    decay_ref[...] = jnp.where(diff >= 0.0, jnp.exp(lg_f * jnp.maximum(diff, 0.0)),
                               jnp.exp(lg_b * jnp.maximum(-diff, 0.0)))
    pos = lax.broadcasted_iota(jnp.int32, (chunk, 1), 0).astype(_f32)
    chunk_len = jnp.full((1, 1), float(chunk), _f32)
    q_decay_f = jnp.exp(lg_f * (pos + 1.0))
    k_decay_f = jnp.exp(lg_f * (chunk - 1.0 - pos))
    gamma_f = jnp.exp(lg_f * chunk_len)
    q_decay_b = jnp.exp(lg_b * (chunk - pos))
    k_decay_b = jnp.exp(lg_b * pos)
    gamma_b = jnp.exp(lg_b * chunk_len)

    def scaled(t, d):
        return (t.astype(_f32) * d).astype(_bf16)

    def kv_outer(kc, vc, d):
        return lax.dot_general(scaled(kc, d), vc, (((0,), (0,)), ((), ())),
                               preferred_element_type=_f32)

    def chunk_rows(c):
        return pl.ds(pl.multiple_of(c * chunk, chunk), chunk)

    s_ref[...] = jnp.zeros_like(s_ref)

    def fwd(c, carry):
        rows = chunk_rows(c)
        snap_ref[c] = s_ref[...].astype(_bf16)
        s_ref[...] = gamma_f * s_ref[...] + kv_outer(k_ref[rows, :], v_ref[rows, :], k_decay_f)
        return carry

    lax.fori_loop(0, n_chunks, fwd, 0, unroll=unroll)

    s_ref[...] = jnp.zeros_like(s_ref)
    w = w_ref[...]

    def bwd(t, carry):
        c = jnp.where(t < n_ctx_chunks, n_ctx_chunks - 1 - t, n_chunks - 1 + n_ctx_chunks - t)
        rows = chunk_rows(c)
        qc, kc, vc = q_ref[rows, :], k_ref[rows, :], v_ref[rows, :]
        s = lax.dot_general(qc, kc, (((1,), (1,)), ((), ())), preferred_element_type=_f32)
        o = jnp.dot((s * decay_ref[...]).astype(_bf16), vc, preferred_element_type=_f32)
        o += q_decay_f * jnp.dot(qc, snap_ref[c], preferred_element_type=_f32)
        o += q_decay_b * jnp.dot(qc, s_ref[...].astype(_bf16), preferred_element_type=_f32)
        s_ref[...] = gamma_b * s_ref[...] + kv_outer(kc, vc, k_decay_b)
        r = o * lax.rsqrt(jnp.mean(o * o, axis=-1, keepdims=True) + EPS) * w
        g = g_ref[rows, :].astype(_f32)
        o_ref[rows, :] = (g * (1.0 / (1.0 + jnp.exp(-g))) * r).astype(o_ref.dtype)
        return carry

    lax.fori_loop(0, n_chunks, bwd, 0, unroll=unroll)


def _retention(qk, rest, lg, norm_w, n_ctx, heads, head_dim):
    m = qk.shape[0]
    chunk = max(c for c in RET_CHUNKS if n_ctx % c == 0 and m % c == 0)
    n_chunks = m // chunk
    ret_width = heads * head_dim
    blk = lambda part: pl.BlockSpec((m, head_dim), lambda h: (0, part * heads + h))
    return pl.pallas_call(
        functools.partial(_retention_kernel, chunk=chunk, n_ctx_chunks=n_ctx // chunk,
                          unroll=max(u for u in (3, 2, 1) if n_chunks % u == 0)),
        out_shape=jax.ShapeDtypeStruct((m, ret_width), _bf16),
        grid=(heads,),
        in_specs=[pl.BlockSpec(memory_space=pltpu.SMEM),
                  blk(0), blk(1), blk(0), blk(1),
                  pl.BlockSpec((1, head_dim), lambda h: (0, h))],
        out_specs=pl.BlockSpec((m, head_dim), lambda h: (0, h)),
        scratch_shapes=[pltpu.VMEM((n_chunks, head_dim, head_dim), _bf16),
                        pltpu.VMEM((head_dim, head_dim), _f32),
                        pltpu.VMEM((chunk, chunk), _f32)],
        compiler_params=_params(("arbitrary",)),
        name="retention",
    )(lg, qk, qk, rest, rest, norm_w.reshape(1, ret_width))


def _conv_kernel(bg_ref, cg_ref, xc_ref, w_ref, o_ref, *, n_ctx, rows_per_step):
    m = o_ref.shape[0]
    r = rows_per_step
    w0, w1, w2 = w_ref[0:1, :], w_ref[1:2, :], w_ref[2:3, :]
    ridx = lax.broadcasted_iota(jnp.int32, (r, 1), 0)

    def u_rows(start, size):
        rows = pl.ds(pl.multiple_of(start, 16), size)
        return cg_ref[rows, :].astype(_f32) * xc_ref[rows, :].astype(_f32)

    def step(i, prev_last):
        r0 = i * r
        u = u_rows(r0, r)
        nxt_start = jnp.minimum(r0 + r, m - 16)
        nxt_first = u_rows(nxt_start, 16)[0:1, :]
        g = r0 + ridx
        is_start = (g == 0) | (g == n_ctx)
        is_end = (g == n_ctx - 1) | (g == m - 1)
        up = jnp.where(ridx == 0, prev_last, pltpu.roll(u, 1, axis=0))
        up = jnp.where(is_start, 0.0, up)
        dn = jnp.where(ridx == r - 1, nxt_first, pltpu.roll(u, r - 1, axis=0))
        dn = jnp.where(is_end, 0.0, dn)
        conv = up * w0 + u * w1 + dn * w2
        rows = pl.ds(pl.multiple_of(r0, 16), r)
        o_ref[rows, :] = (bg_ref[rows, :].astype(_f32) * conv).astype(o_ref.dtype)
        return u[r - 1:r, :]

    lax.fori_loop(0, m // r, step, jnp.zeros((1, o_ref.shape[1]), _f32))


def _gated_conv(y, conv_w_t, n_ctx, ret_width, conv_width):
    m = y.shape[0]
    bc = _block(conv_width, 256, 128)
    nb = conv_width // bc
    base = 2 * ret_width // bc
    r = _block(n_ctx, 128, 16)
    blk = lambda part: pl.BlockSpec((m, bc), lambda j: (0, base + part * nb + j))
    return pl.pallas_call(
        functools.partial(_conv_kernel, n_ctx=n_ctx, rows_per_step=r),
        out_shape=jax.ShapeDtypeStruct((m, conv_width), _bf16),
        grid=(nb,),
        in_specs=[blk(0), blk(1), blk(2), pl.BlockSpec((3, bc), lambda j: (0, j))],
        out_specs=pl.BlockSpec((m, bc), lambda j: (0, j)),
        compiler_params=_params(("arbitrary",)),
        name="gated_conv",
    )(y, y, y, conv_w_t)


def _out_proj_kernel(r_ref, c_ref, w_hbm, h_ref, g_ref, o_ref, stage_ref, wb_ref, sem, *,
                     layer, n_ctx):
    rows, kr = r_ref.shape
    bn = wb_ref.shape[1]

    @pl.when(pl.program_id(1) == 0)
    def _():
        _fetch_weight_block(w_hbm, stage_ref, wb_ref, sem, layer, pl.program_id(0),
                            pl.num_programs(0), lambda t: (0, t * bn))

    y = (jnp.dot(r_ref[...], wb_ref[0:kr, :], preferred_element_type=_f32)
         + jnp.dot(c_ref[...], wb_ref[kr:, :], preferred_element_type=_f32))
    gate = _row_select(g_ref, pl.program_id(1) * rows, rows, n_ctx)
    o_ref[...] = h_ref[...] + gate * y


def _out_proj(ret, conv, w, layer, h, gate, n_ctx):
    m, kr = ret.shape
    kc = conv.shape[1]
    n = w.shape[2]
    bm = _block(m, 704, 16)
    bn = _block(n, 1024, 128)
    return pl.pallas_call(
        functools.partial(_out_proj_kernel, layer=layer, n_ctx=n_ctx),
        out_shape=jax.ShapeDtypeStruct((m, n), _f32),
        grid=(n // bn, m // bm),
        in_specs=[pl.BlockSpec((bm, kr), lambda j, i: (i, 0)),
                  pl.BlockSpec((bm, kc), lambda j, i: (i, 0)),
                  pl.BlockSpec(memory_space=pl.ANY),
                  pl.BlockSpec((bm, bn), lambda j, i: (i, j)),
                  pl.BlockSpec((ADA_ROWS, bn), lambda j, i: (0, j))],
        out_specs=pl.BlockSpec((bm, bn), lambda j, i: (i, j)),
        scratch_shapes=_weight_scratch(kr + kc, bn),
        compiler_params=_params(("arbitrary", "arbitrary")),
        name="out_proj",
    )(ret, conv, w, h, gate)


def _mlp_up_kernel(a_ref, w_hbm, o_ref, stage_ref, wb_ref, sem, *, layer):
    bn = wb_ref.shape[1]

    @pl.when(pl.program_id(1) == 0)
    def _():
        _fetch_weight_block(w_hbm, stage_ref, wb_ref, sem, layer, pl.program_id(0),
                            pl.num_programs(0), lambda t: (0, t * bn))

    z = jnp.maximum(jnp.dot(a_ref[...], wb_ref[...], preferred_element_type=_f32), 0.0)
    o_ref[...] = (z * z).astype(o_ref.dtype)


def _mlp_up(a, w, layer):
    m, k = a.shape
    n = w.shape[2]
    bm = _block(m, 1056, 16)
    bn = _block(n, 1024, 128)
    return pl.pallas_call(
        functools.partial(_mlp_up_kernel, layer=layer),
        out_shape=jax.ShapeDtypeStruct((m, n), _bf16),
        grid=(n // bn, m // bm),
        in_specs=[pl.BlockSpec((bm, k), lambda j, i: (i, 0)),
                  pl.BlockSpec(memory_space=pl.ANY)],
        out_specs=pl.BlockSpec((bm, bn), lambda j, i: (i, j)),
        scratch_shapes=_weight_scratch(k, bn),
        compiler_params=_params(("arbitrary", "arbitrary")),
        name="mlp_up",
    )(a, w)


def _mlp_down_kernel(z_ref, w_hbm, h_ref, g_ref, o_ref, stage_ref, wb_ref, sem, acc_ref, *,
                     layer, n_ctx, nk):
    j, kk, i = pl.program_id(0), pl.program_id(1), pl.program_id(2)
    bm = z_ref.shape[0]
    bk, bn = wb_ref.shape

    @pl.when(i == 0)
    def _():
        _fetch_weight_block(w_hbm, stage_ref, wb_ref, sem, layer, j * nk + kk,
                            pl.num_programs(0) * nk, lambda t: ((t % nk) * bk, (t // nk) * bn))

    rows = pl.ds(pl.multiple_of(i * bm, bm), bm)

    def partial_product():
        return jnp.dot(z_ref[...], wb_ref[...], preferred_element_type=_f32)

    def finish(total):
        gate = _row_select(g_ref, i * bm, bm, n_ctx)
        o_ref[...] = h_ref[...] + gate * total

    if nk == 1:
        finish(partial_product())
        return

    @pl.when(kk == 0)
    def _():
        acc_ref[rows, :] = partial_product()

    if nk > 2:
        @pl.when((kk > 0) & (kk < nk - 1))
        def _():
            acc_ref[rows, :] += partial_product()

    @pl.when(kk == nk - 1)
    def _():
        finish(acc_ref[rows, :] + partial_product())


def _mlp_down(z, w, layer, h, gate, n_ctx):
    m, k = z.shape
    n = w.shape[2]
    bm = _block(m, 1056, 16)
    bn = _block(n, 512, 128)
    bk = _block(k, 4096, CAST_ROWS)
    nk = k // bk
    res_map = lambda j, kk, i: (jnp.where(kk == nk - 1, i, 0), j)
    return pl.pallas_call(
        functools.partial(_mlp_down_kernel, layer=layer, n_ctx=n_ctx, nk=nk),
        out_shape=jax.ShapeDtypeStruct((m, n), _f32),
        grid=(n // bn, nk, m // bm),
        in_specs=[pl.BlockSpec((bm, bk), lambda j, kk, i: (i, kk)),
                  pl.BlockSpec(memory_space=pl.ANY),
                  pl.BlockSpec((bm, bn), res_map),
                  pl.BlockSpec((ADA_ROWS, bn), lambda j, kk, i: (0, j))],
        out_specs=pl.BlockSpec((bm, bn), res_map),
        scratch_shapes=_weight_scratch(bk, bn) + [pltpu.VMEM((m, bn), _f32)],
        compiler_params=_params(("arbitrary", "arbitrary", "arbitrary")),
        name="mlp_down",
    )(z, w, h, gate)


def _rope_tables(n_lat, n_ctx, head_dim):
    rows = n_lat // GRID_W
    row = jnp.broadcast_to(jnp.arange(rows, dtype=_f32)[:, None], (rows, GRID_W)).reshape(-1)
    col = jnp.broadcast_to(jnp.arange(GRID_W, dtype=_f32)[None, :], (rows, GRID_W)).reshape(-1)
    nf = head_dim // 4
    inv = ROPE_THETA ** (-jnp.arange(nf, dtype=_f32) / nf)
    ang = jnp.concatenate([row[:, None] * inv, col[:, None] * inv], axis=-1)
    cos = jnp.concatenate([jnp.ones((n_ctx, 2 * nf), _f32), jnp.cos(ang)], axis=0)
    sin = jnp.concatenate([jnp.zeros((n_ctx, 2 * nf), _f32), jnp.sin(ang)], axis=0)
    return cos, sin


def kernel(x, c, ctx, c_ctx, ada_w_down, ada_w_up, ada_b, norm1_w, norm2_w, w_in, ret_decay_fwd,
           ret_decay_bwd, ret_norm_w, conv_w, w_out, mlp_w1, mlp_w2, final_norm_w):
    batch, n_lat, d = x.shape
    n_ctx = ctx.shape[1]
    depth = w_in.shape[0]
    heads = ret_decay_fwd.shape[1]
    ret_width = ret_norm_w.shape[1]
    conv_width = conv_w.shape[1]
    head_dim = ret_width // heads
    assert batch == 1 and c.shape[0] == 1
    assert n_ctx % CHUNK == 0 and n_lat % CHUNK == 0 and n_lat % GRID_W == 0
    assert w_in.shape[2] == 4 * ret_width + 3 * conv_width and ret_width + conv_width == d

    s = jnp.zeros((ADA_ROWS, d), _f32).at[0].set(jax.nn.silu(c[0])).at[1].set(jax.nn.silu(c_ctx))
    mods = _ada_mod(s, ada_w_down, ada_w_up, ada_b)
    cos, sin = _rope_tables(n_lat, n_ctx, head_dim)
    lg = jnp.stack([jax.nn.log_sigmoid(ret_decay_fwd.astype(_f32)),
                    jax.nn.log_sigmoid(ret_decay_bwd.astype(_f32))], axis=1)

    h = None
    for l in range(depth):
        sh1, sc1, g1, sh2, sc2, g2 = [mods[l, :, j * d:(j + 1) * d] for j in range(N_MOD)]
        if l == 0:
            h, a = _norm_mod_first(ctx[0], x[0], norm1_w[l], sh1, sc1)
        else:
            a = _norm_mod(h, norm1_w[l], sh1, sc1, n_ctx)
        qk, rest = _in_proj(a, w_in, l, cos, sin, ret_width, head_dim)
        ret = _retention(qk, rest, lg[l], ret_norm_w[l], n_ctx, heads, head_dim)
        conv = _gated_conv(rest, conv_w[l].T, n_ctx, ret_width, conv_width)
        h = _out_proj(ret, conv, w_out, l, h, g1, n_ctx)
        a = _norm_mod(h, norm2_w[l], sh2, sc2, n_ctx)
        z = _mlp_up(a, mlp_w1, l)
        h = _mlp_down(z, mlp_w2, l, h, g2, n_ctx)
    return _final_norm(h, final_norm_w, n_ctx)[None]
```

```python
import functools

import jax
import jax.numpy as jnp
from jax import lax
from jax.experimental import pallas as pl
from jax.experimental.pallas import tpu as pltpu

GRID_W = 64
CHUNK = 128
ROPE_THETA = 10000.0
EPS = 1e-6
N_MOD = 6
ADA_ROWS = 8
NORM_GROUP = 16
CAST_ROWS = 256
RET_CHUNKS = (256, 128)
V7X_VMEM_LIMIT_BYTES = 60 * 1024 * 1024

_bf16 = jnp.bfloat16
_f32 = jnp.float32


def _block(total, target, align):
    best = None
    for d in range(align, min(total, target) + 1, align):
        if total % d == 0:
            best = d
    if best is None:
        raise ValueError(f"no block for {total} (target {target}, align {align})")
    return best


def _params(sem):
    return pltpu.CompilerParams(dimension_semantics=sem, vmem_limit_bytes=V7X_VMEM_LIMIT_BYTES)


def _fetch_weight_block(w_hbm, stage_ref, wb_ref, sem, layer, t, n_blocks, origin):
    rows, cols = stage_ref.shape

    def aligned(v, a):
        return v if isinstance(v, int) else pl.multiple_of(v, a)

    def copy(tt):
        r0, c0 = origin(tt)
        src = w_hbm.at[layer, pl.ds(aligned(r0, rows), rows), pl.ds(aligned(c0, cols), cols)]
        return pltpu.make_async_copy(src, stage_ref, sem.at[0])

    @pl.when(t == 0)
    def _():
        copy(t).start()

    copy(t).wait()

    def cast(r, carry):
        sl = pl.ds(pl.multiple_of(r * CAST_ROWS, CAST_ROWS), CAST_ROWS)
        wb_ref[sl, :] = stage_ref[sl, :].astype(_bf16)
        return carry

    lax.fori_loop(0, rows // CAST_ROWS, cast, 0)

    @pl.when(t + 1 < n_blocks)
    def _():
        copy(t + 1).start()


def _weight_scratch(bk, bn):
    assert bk % CAST_ROWS == 0
    return [pltpu.VMEM((bk, bn), _f32), pltpu.VMEM((bk, bn), _bf16), pltpu.SemaphoreType.DMA((1,))]


def _ada_down_kernel(s_ref, wd_ref, t_ref):
    t_ref[0] = jnp.dot(s_ref[...].astype(_bf16), wd_ref[0].astype(_bf16),
                       preferred_element_type=_f32)


def _ada_up_kernel(t_ref, wu_ref, b_ref, m_ref):
    m_ref[0] = jnp.dot(t_ref[0].astype(_bf16), wu_ref[0].astype(_bf16),
                       preferred_element_type=_f32) + b_ref[0]


def _ada_mod(s, w_down, w_up, b):
    depth, d, rank = w_down.shape
    n = w_up.shape[2]
    t = pl.pallas_call(
        _ada_down_kernel,
        out_shape=jax.ShapeDtypeStruct((depth, ADA_ROWS, rank), _f32),
        grid=(depth,),
        in_specs=[pl.BlockSpec((ADA_ROWS, d), lambda l: (0, 0)),
                  pl.BlockSpec((1, d, rank), lambda l: (l, 0, 0))],
        out_specs=pl.BlockSpec((1, ADA_ROWS, rank), lambda l: (l, 0, 0)),
        compiler_params=_params(("arbitrary",)),
        name="ada_down",
    )(s, w_down)
    bn = _block(n, 4096, 128)
    return pl.pallas_call(
        _ada_up_kernel,
        out_shape=jax.ShapeDtypeStruct((depth, ADA_ROWS, n), _f32),
        grid=(depth, n // bn),
        in_specs=[pl.BlockSpec((1, ADA_ROWS, rank), lambda l, j: (l, 0, 0)),
                  pl.BlockSpec((1, rank, bn), lambda l, j: (l, 0, j)),
                  pl.BlockSpec((1, 1, bn), lambda l, j: (l, 0, j))],
        out_specs=pl.BlockSpec((1, ADA_ROWS, bn), lambda l, j: (l, 0, j)),
        compiler_params=_params(("arbitrary", "arbitrary")),
        name="ada_up",
    )(t, w_up, b.reshape(depth, 1, n))


def _row_select(mod_ref, row0, rows, n_lat):
    ridx = row0 + lax.broadcasted_iota(jnp.int32, (rows, 1), 0)
    return jnp.where(ridx < n_lat, mod_ref[0:1, :], mod_ref[1:2, :])


def _norm_mod_kernel(x_ref, w_ref, sh_ref, sc_ref, o_ref, gain_ref, shift_ref, rstd_ref, *, n_lat):
    rows, d = x_ref.shape
    g_rows = NORM_GROUP
    w = w_ref[...]
    for r in range(2):
        gain_ref[r * g_rows:(r + 1) * g_rows, :] = jnp.broadcast_to(
            w * (1.0 + sc_ref[r:r + 1, :]), (g_rows, d))
        shift_ref[r * g_rows:(r + 1) * g_rows, :] = jnp.broadcast_to(sh_ref[r:r + 1, :], (g_rows, d))

    def group(g):
        return pl.ds(pl.multiple_of(g * g_rows, g_rows), g_rows)

    def stats(g, carry):
        x = x_ref[group(g), :]
        rstd_ref[group(g), :] = lax.rsqrt(jnp.mean(x * x, axis=-1, keepdims=True) + EPS)
        return carry

    lax.fori_loop(0, rows // g_rows, stats, 0, unroll=8)

    lat_groups = jnp.clip(n_lat - pl.program_id(0) * rows, 0, rows) // g_rows

    def apply(g, carry):
        mod = pl.ds(pl.multiple_of(jnp.where(g < lat_groups, 0, g_rows), g_rows), g_rows)
        x = x_ref[group(g), :]
        o_ref[group(g), :] = ((x * rstd_ref[group(g), :]) * gain_ref[mod, :]
                              + shift_ref[mod, :]).astype(o_ref.dtype)
        return carry

    lax.fori_loop(0, rows // g_rows, apply, 0, unroll=4)


def _norm_scratch(br, d):
    return [pltpu.VMEM((2 * NORM_GROUP, d), _f32), pltpu.VMEM((2 * NORM_GROUP, d), _f32),
            pltpu.VMEM((br, 1), _f32)]


def _norm_mod_first_kernel(x_ref, ctx_ref, w_ref, sh_ref, sc_ref, h_ref, o_ref, gain_ref,
                           shift_ref, rstd_ref, *, n_lat):
    in_lat = pl.program_id(0) * h_ref.shape[0] < n_lat

    @pl.when(in_lat)
    def _():
        h_ref[...] = x_ref[...]

    @pl.when(jnp.logical_not(in_lat))
    def _():
        h_ref[...] = ctx_ref[...]

    _norm_mod_kernel(h_ref, w_ref, sh_ref, sc_ref, o_ref, gain_ref, shift_ref, rstd_ref,
                     n_lat=n_lat)


def _norm_mod_first(x, ctx, w, sh, sc):
    n_lat, d = x.shape
    n_ctx = ctx.shape[0]
    m = n_lat + n_ctx
    br = _block(n_ctx, 512, 8 * NORM_GROUP)
    assert n_lat % br == 0
    lat_blocks = n_lat // br
    row_spec = pl.BlockSpec((br, d), lambda i: (i, 0))
    vec_spec = pl.BlockSpec((ADA_ROWS, d), lambda i: (0, 0))
    return pl.pallas_call(
        functools.partial(_norm_mod_first_kernel, n_lat=n_lat),
        out_shape=(jax.ShapeDtypeStruct((m, d), _f32), jax.ShapeDtypeStruct((m, d), _bf16)),
        grid=(m // br,),
        in_specs=[pl.BlockSpec((br, d), lambda i: (jnp.minimum(i, lat_blocks - 1), 0)),
                  pl.BlockSpec((br, d), lambda i: (jnp.maximum(i - lat_blocks, 0), 0)),
                  pl.BlockSpec((1, d), lambda i: (0, 0)), vec_spec, vec_spec],
        out_specs=(row_spec, row_spec),
        scratch_shapes=_norm_scratch(br, d),
        compiler_params=_params(("arbitrary",)),
        name="norm_mod_first",
    )(x, ctx, w.reshape(1, d), sh, sc)


def _norm_mod(h, w, sh, sc, n_lat):
    m, d = h.shape
    br = _block(m, 512, 8 * NORM_GROUP)
    assert n_lat % NORM_GROUP == 0
    return pl.pallas_call(
        functools.partial(_norm_mod_kernel, n_lat=n_lat),
        out_shape=jax.ShapeDtypeStruct((m, d), _bf16),
        grid=(m // br,),
        in_specs=[pl.BlockSpec((br, d), lambda i: (i, 0)),
                  pl.BlockSpec((1, d), lambda i: (0, 0)),
                  pl.BlockSpec((ADA_ROWS, d), lambda i: (0, 0)),
                  pl.BlockSpec((ADA_ROWS, d), lambda i: (0, 0))],
        out_specs=pl.BlockSpec((br, d), lambda i: (i, 0)),
        scratch_shapes=_norm_scratch(br, d),
        compiler_params=_params(("arbitrary",)),
        name="norm_mod",
    )(h, w.reshape(1, d), sh, sc)


def _final_norm_kernel(x_ref, w_ref, o_ref):
    x = x_ref[...]
    o_ref[...] = x * lax.rsqrt(jnp.mean(x * x, axis=-1, keepdims=True) + EPS) * w_ref[...]


def _final_norm(h, w, n_lat):
    d = h.shape[1]
    br = _block(n_lat, 256, 8)
    return pl.pallas_call(
        _final_norm_kernel,
        out_shape=jax.ShapeDtypeStruct((n_lat, d), _f32),
        grid=(n_lat // br,),
        in_specs=[pl.BlockSpec((br, d), lambda i: (i, 0)),
                  pl.BlockSpec((1, d), lambda i: (0, 0))],
        out_specs=pl.BlockSpec((br, d), lambda i: (i, 0)),
        compiler_params=_params(("arbitrary",)),
        name="final_norm",
    )(h, w.reshape(1, d))


def _in_proj_rope_kernel(a_ref, w_hbm, cos_ref, sin_ref, o_ref, stage_ref, wb_ref, sem, *,
                         layer, n_q_blocks, head_dim):
    j = pl.program_id(0)
    bn = wb_ref.shape[1]

    @pl.when(pl.program_id(1) == 0)
    def _():
        _fetch_weight_block(w_hbm, stage_ref, wb_ref, sem, layer, j, pl.num_programs(0),
                            lambda t: (0, t * bn))

    y = jnp.dot(a_ref[...], wb_ref[...], preferred_element_type=_f32)
    half = head_dim // 2
    scale = jnp.where(j < n_q_blocks, head_dim ** -0.5, 1.0).astype(_f32)
    cos = cos_ref[...] * scale
    sin = sin_ref[...] * scale
    for hd in range(bn // head_dim):
        c0 = hd * head_dim
        t1 = y[:, c0:c0 + half]
        t2 = y[:, c0 + half:c0 + head_dim]
        o_ref[:, c0:c0 + half] = (t1 * cos - t2 * sin).astype(o_ref.dtype)
        o_ref[:, c0 + half:c0 + head_dim] = (t1 * sin + t2 * cos).astype(o_ref.dtype)


def _in_proj_plain_kernel(a_ref, w_hbm, o_ref, stage_ref, wb_ref, sem, *, layer, col0):
    bn = wb_ref.shape[1]

    @pl.when(pl.program_id(1) == 0)
    def _():
        _fetch_weight_block(w_hbm, stage_ref, wb_ref, sem, layer, pl.program_id(0),
                            pl.num_programs(0), lambda t: (0, col0 + t * bn))

    o_ref[...] = jnp.dot(a_ref[...], wb_ref[...], preferred_element_type=_f32).astype(o_ref.dtype)


def _in_proj(a, w, layer, cos, sin, ret_width, head_dim):
    m, k = a.shape
    n = w.shape[2]
    bm = _block(m, 1056, 16)
    bn = _block(ret_width, 1024, head_dim)
    n_rope = 2 * ret_width
    assert (n - n_rope) % bn == 0
    half = head_dim // 2
    a_spec = pl.BlockSpec((bm, k), lambda j, i: (i, 0))
    o_spec = pl.BlockSpec((bm, bn), lambda j, i: (i, j))
    qk = pl.pallas_call(
        functools.partial(_in_proj_rope_kernel, layer=layer, n_q_blocks=ret_width // bn,
                          head_dim=head_dim),
        out_shape=jax.ShapeDtypeStruct((m, n_rope), _bf16),
        grid=(n_rope // bn, m // bm),
        in_specs=[a_spec, pl.BlockSpec(memory_space=pl.ANY),
                  pl.BlockSpec((bm, half), lambda j, i: (i, 0)),
                  pl.BlockSpec((bm, half), lambda j, i: (i, 0))],
        out_specs=o_spec,
        scratch_shapes=_weight_scratch(k, bn),
        compiler_params=_params(("arbitrary", "arbitrary")),
        name="in_proj_qk",
    )(a, w, cos, sin)
    rest = pl.pallas_call(
        functools.partial(_in_proj_plain_kernel, layer=layer, col0=n_rope),
        out_shape=jax.ShapeDtypeStruct((m, n - n_rope), _bf16),
        grid=((n - n_rope) // bn, m // bm),
        in_specs=[a_spec, pl.BlockSpec(memory_space=pl.ANY)],
        out_specs=o_spec,
        scratch_shapes=_weight_scratch(k, bn),
        compiler_params=_params(("arbitrary", "arbitrary")),
        name="in_proj_rest",
    )(a, w)
    return qk, rest


def _retention_kernel(lg_ref, q_ref, k_ref, v_ref, g_ref, w_ref, o_ref, snap_ref, s_ref,
                      decay_ref, *, chunk, n_ctx_chunks, unroll):
    head = pl.program_id(0)
    n_chunks = q_ref.shape[0] // chunk
    n_lat_chunks = n_chunks - n_ctx_chunks
    lg_f = lg_ref[0, head]
    lg_b = lg_ref[1, head]

    ii = lax.broadcasted_iota(jnp.int32, (chunk, chunk), 0)
    jj = lax.broadcasted_iota(jnp.int32, (chunk, chunk), 1)
    diff = (ii - jj).astype(_f32)
    decay_ref[...] = jnp.where(diff >= 0.0, jnp.exp(lg_f * jnp.maximum(diff, 0.0)),
                               jnp.exp(lg_b * jnp.maximum(-diff, 0.0)))
    pos = lax.broadcasted_iota(jnp.int32, (chunk, 1), 0).astype(_f32)
    chunk_len = jnp.full((1, 1), float(chunk), _f32)
    q_decay_f = jnp.exp(lg_f * (pos + 1.0))
    k_decay_f = jnp.exp(lg_f * (chunk - 1.0 - pos))
    gamma_f = jnp.exp(lg_f * chunk_len)
    q_decay_b = jnp.exp(lg_b * (chunk - pos))
    k_decay_b = jnp.exp(lg_b * pos)
    gamma_b = jnp.exp(lg_b * chunk_len)

    def scaled(t, d):
        return (t.astype(_f32) * d).astype(_bf16)

    def kv_outer(kc, vc, d):
        return lax.dot_general(scaled(kc, d), vc, (((0,), (0,)), ((), ())),
                               preferred_element_type=_f32)

    def chunk_rows(c):
        return pl.ds(pl.multiple_of(c * chunk, chunk), chunk)

    s_ref[...] = jnp.zeros_like(s_ref)

    def fwd(t, carry):
        c = jnp.where(t < n_ctx_chunks, t + n_lat_chunks, t - n_ctx_chunks)
        rows = chunk_rows(c)
        snap_ref[c] = s_ref[...].astype(_bf16)
        s_ref[...] = gamma_f * s_ref[...] + kv_outer(k_ref[rows, :], v_ref[rows, :], k_decay_f)
        return carry

    lax.fori_loop(0, n_chunks, fwd, 0, unroll=unroll)

    s_ref[...] = jnp.zeros_like(s_ref)
    w = w_ref[...]

    def bwd(t, carry):
        c = n_chunks - 1 - t
        rows = chunk_rows(c)
        qc, kc, vc = q_ref[rows, :], k_ref[rows, :], v_ref[rows, :]
        s = lax.dot_general(qc, kc, (((1,), (1,)), ((), ())), preferred_element_type=_f32)
        o = jnp.dot((s * decay_ref[...]).astype(_bf16), vc, preferred_element_type=_f32)
        o += q_decay_f * jnp.dot(qc, snap_ref[c], preferred_element_type=_f32)
        o += q_decay_b * jnp.dot(qc, s_ref[...].astype(_bf16), preferred_element_type=_f32)
        s_ref[...] = gamma_b * s_ref[...] + kv_outer(kc, vc, k_decay_b)
        r = o * lax.rsqrt(jnp.mean(o * o, axis=-1, keepdims=True) + EPS) * w
        g = g_ref[rows, :].astype(_f32)
        o_ref[rows, :] = (g * (1.0 / (1.0 + jnp.exp(-g))) * r).astype(o_ref.dtype)
        return carry

    lax.fori_loop(0, n_chunks, bwd, 0, unroll=unroll)


def _retention(qk, rest, lg, norm_w, n_ctx, heads, head_dim):
    m = qk.shape[0]
    chunk = max(c for c in RET_CHUNKS if n_ctx % c == 0 and m % c == 0)
    n_chunks = m // chunk
    ret_width = heads * head_dim
    blk = lambda part: pl.BlockSpec((m, head_dim), lambda h: (0, part * heads + h))
    return pl.pallas_call(
        functools.partial(_retention_kernel, chunk=chunk, n_ctx_chunks=n_ctx // chunk,
                          unroll=max(u for u in (3, 2, 1) if n_chunks % u == 0)),
        out_shape=jax.ShapeDtypeStruct((m, ret_width), _bf16),
        grid=(heads,),
        in_specs=[pl.BlockSpec(memory_space=pltpu.SMEM),
                  blk(0), blk(1), blk(0), blk(1),
                  pl.BlockSpec((1, head_dim), lambda h: (0, h))],
        out_specs=pl.BlockSpec((m, head_dim), lambda h: (0, h)),
        scratch_shapes=[pltpu.VMEM((n_chunks, head_dim, head_dim), _bf16),
                        pltpu.VMEM((head_dim, head_dim), _f32),
                        pltpu.VMEM((chunk, chunk), _f32)],
        compiler_params=_params(("arbitrary",)),
        name="retention",
    )(lg, qk, qk, rest, rest, norm_w.reshape(1, ret_width))


def _conv_kernel(bg_ref, cg_ref, xc_ref, w_ref, o_ref, *, n_lat, rows_per_step):
    m = o_ref.shape[0]
    r = rows_per_step
    w0, w1, w2 = w_ref[0:1, :], w_ref[1:2, :], w_ref[2:3, :]
    ridx = lax.broadcasted_iota(jnp.int32, (r, 1), 0)

    def u_rows(start, size):
        rows = pl.ds(pl.multiple_of(start, 16), size)
        return cg_ref[rows, :].astype(_f32) * xc_ref[rows, :].astype(_f32)

    def step(i, prev_last):
        r0 = i * r
        u = u_rows(r0, r)
        nxt_first = u_rows(jnp.minimum(r0 + r, m - 16), 16)[0:1, :]
        prev_row = jnp.where((r0 == 0) | (r0 == n_lat), 0.0, prev_last)
        next_row = jnp.where((r0 + r == n_lat) | (r0 + r == m), 0.0, nxt_first)
        up = jnp.where(ridx == 0, prev_row, pltpu.roll(u, 1, axis=0))
        dn = jnp.where(ridx == r - 1, next_row, pltpu.roll(u, r - 1, axis=0))
        conv = up * w0 + u * w1 + dn * w2
        rows = pl.ds(pl.multiple_of(r0, 16), r)
        o_ref[rows, :] = (bg_ref[rows, :].astype(_f32) * conv).astype(o_ref.dtype)
        return u[r - 1:r, :]

    lax.fori_loop(0, m // r, step, jnp.zeros((1, o_ref.shape[1]), _f32))


def _gated_conv(y, conv_w_t, n_lat, ret_width, conv_width):
    m = y.shape[0]
    bc = _block(conv_width, 256, 128)
    nb = conv_width // bc
    base = 2 * ret_width // bc
    r = _block(m - n_lat, 128, 16)
    assert n_lat % r == 0
    blk = lambda part: pl.BlockSpec((m, bc), lambda j: (0, base + part * nb + j))
    return pl.pallas_call(
        functools.partial(_conv_kernel, n_lat=n_lat, rows_per_step=r),
        out_shape=jax.ShapeDtypeStruct((m, conv_width), _bf16),
        grid=(nb,),
        in_specs=[blk(0), blk(1), blk(2), pl.BlockSpec((3, bc), lambda j: (0, j))],
        out_specs=pl.BlockSpec((m, bc), lambda j: (0, j)),
        compiler_params=_params(("arbitrary",)),
        name="gated_conv",
    )(y, y, y, conv_w_t)


def _out_proj_kernel(r_ref, c_ref, w_hbm, h_ref, g_ref, o_ref, stage_ref, wb_ref, sem, *,
                     layer, n_lat):
    rows, kr = r_ref.shape
    bn = wb_ref.shape[1]

    @pl.when(pl.program_id(1) == 0)
    def _():
        _fetch_weight_block(w_hbm, stage_ref, wb_ref, sem, layer, pl.program_id(0),
                            pl.num_programs(0), lambda t: (0, t * bn))

    y = (jnp.dot(r_ref[...], wb_ref[0:kr, :], preferred_element_type=_f32)
         + jnp.dot(c_ref[...], wb_ref[kr:, :], preferred_element_type=_f32))
    gate = _row_select(g_ref, pl.program_id(1) * rows, rows, n_lat)
    o_ref[...] = h_ref[...] + gate * y


def _out_proj(ret, conv, w, layer, h, gate, n_lat, m):
    kr = ret.shape[1]
    kc = conv.shape[1]
    n = w.shape[2]
    bm = _block(m, 704, 16)
    bn = _block(n, 1024, 128)
    return pl.pallas_call(
        functools.partial(_out_proj_kernel, layer=layer, n_lat=n_lat),
        out_shape=jax.ShapeDtypeStruct((m, n), _f32),
        grid=(n // bn, m // bm),
        in_specs=[pl.BlockSpec((bm, kr), lambda j, i: (i, 0)),
                  pl.BlockSpec((bm, kc), lambda j, i: (i, 0)),
                  pl.BlockSpec(memory_space=pl.ANY),
                  pl.BlockSpec((bm, bn), lambda j, i: (i, j)),
                  pl.BlockSpec((ADA_ROWS, bn), lambda j, i: (0, j))],
        out_specs=pl.BlockSpec((bm, bn), lambda j, i: (i, j)),
        scratch_shapes=_weight_scratch(kr + kc, bn),
        compiler_params=_params(("arbitrary", "arbitrary")),
        name="out_proj",
    )(ret, conv, w, h, gate)


def _mlp_up_kernel(a_ref, w_hbm, o_ref, stage_ref, wb_ref, sem, *, layer):
    bn = wb_ref.shape[1]

    @pl.when(pl.program_id(1) == 0)
    def _():
        _fetch_weight_block(w_hbm, stage_ref, wb_ref, sem, layer, pl.program_id(0),
                            pl.num_programs(0), lambda t: (0, t * bn))

    z = jnp.maximum(jnp.dot(a_ref[...], wb_ref[...], preferred_element_type=_f32), 0.0)
    o_ref[...] = (z * z).astype(o_ref.dtype)


def _mlp_up(a, w, layer):
    m, k = a.shape
    n = w.shape[2]
    bm = _block(m, 1056, 16)
    bn = _block(n, 1024, 128)
    return pl.pallas_call(
        functools.partial(_mlp_up_kernel, layer=layer),
        out_shape=jax.ShapeDtypeStruct((m, n), _bf16),
        grid=(n // bn, m // bm),
        in_specs=[pl.BlockSpec((bm, k), lambda j, i: (i, 0)),
                  pl.BlockSpec(memory_space=pl.ANY)],
        out_specs=pl.BlockSpec((bm, bn), lambda j, i: (i, j)),
        scratch_shapes=_weight_scratch(k, bn),
        compiler_params=_params(("arbitrary", "arbitrary")),
        name="mlp_up",
    )(a, w)


def _mlp_down_kernel(z_ref, w_hbm, h_ref, g_ref, o_ref, stage_ref, wb_ref, sem, acc_ref, *,
                     layer, n_lat, nk):
    j, kk, i = pl.program_id(0), pl.program_id(1), pl.program_id(2)
    bm = z_ref.shape[0]
    bk, bn = wb_ref.shape

    @pl.when(i == 0)
    def _():
        _fetch_weight_block(w_hbm, stage_ref, wb_ref, sem, layer, j * nk + kk,
                            pl.num_programs(0) * nk, lambda t: ((t % nk) * bk, (t // nk) * bn))

    rows = pl.ds(pl.multiple_of(i * bm, bm), bm)

    def partial_product():
        return jnp.dot(z_ref[...], wb_ref[...], preferred_element_type=_f32)

    def finish(total):
        gate = _row_select(g_ref, i * bm, bm, n_lat)
        o_ref[...] = h_ref[...] + gate * total

    if nk == 1:
        finish(partial_product())
        return

    @pl.when(kk == 0)
    def _():
        acc_ref[rows, :] = partial_product()

    if nk > 2:
        @pl.when((kk > 0) & (kk < nk - 1))
        def _():
            acc_ref[rows, :] += partial_product()

    @pl.when(kk == nk - 1)
    def _():
        finish(acc_ref[rows, :] + partial_product())


def _mlp_down(z, w, layer, h, gate, n_lat):
    m, k = z.shape
    n = w.shape[2]
    bm = _block(m, 1056, 16)
    bn = _block(n, 512, 128)
    bk = _block(k, 4096, CAST_ROWS)
    nk = k // bk
    res_map = lambda j, kk, i: (jnp.where(kk == nk - 1, i, 0), j)
    return pl.pallas_call(
        functools.partial(_mlp_down_kernel, layer=layer, n_lat=n_lat, nk=nk),
        out_shape=jax.ShapeDtypeStruct((m, n), _f32),
        grid=(n // bn, nk, m // bm),
        in_specs=[pl.BlockSpec((bm, bk), lambda j, kk, i: (i, kk)),
                  pl.BlockSpec(memory_space=pl.ANY),
                  pl.BlockSpec((bm, bn), res_map),
                  pl.BlockSpec((ADA_ROWS, bn), lambda j, kk, i: (0, j))],
        out_specs=pl.BlockSpec((bm, bn), res_map),
        scratch_shapes=_weight_scratch(bk, bn) + [pltpu.VMEM((m, bn), _f32)],
        compiler_params=_params(("arbitrary", "arbitrary", "arbitrary")),
        name="mlp_down",
    )(z, w, h, gate)


def _rope_tables(n_lat, n_ctx, head_dim):
    rows = n_lat // GRID_W
    row = jnp.broadcast_to(jnp.arange(rows, dtype=_f32)[:, None], (rows, GRID_W)).reshape(-1)
    col = jnp.broadcast_to(jnp.arange(GRID_W, dtype=_f32)[None, :], (rows, GRID_W)).reshape(-1)
    nf = head_dim // 4
    inv = ROPE_THETA ** (-jnp.arange(nf, dtype=_f32) / nf)
    ang = jnp.concatenate([row[:, None] * inv, col[:, None] * inv], axis=-1)
    cos = jnp.concatenate([jnp.cos(ang), jnp.ones((n_ctx, 2 * nf), _f32)], axis=0)
    sin = jnp.concatenate([jnp.sin(ang), jnp.zeros((n_ctx, 2 * nf), _f32)], axis=0)
    return cos, sin


def kernel(x, c, ctx, c_ctx, ada_w_down, ada_w_up, ada_b, norm1_w, norm2_w, w_in, ret_decay_fwd,
           ret_decay_bwd, ret_norm_w, conv_w, w_out, mlp_w1, mlp_w2, final_norm_w):
    batch, n_lat, d = x.shape
    n_ctx = ctx.shape[1]
    depth = w_in.shape[0]
    heads = ret_decay_fwd.shape[1]
    ret_width = ret_norm_w.shape[1]
    conv_width = conv_w.shape[1]
    head_dim = ret_width // heads
    assert batch == 1 and c.shape[0] == 1
    assert n_ctx % CHUNK == 0 and n_lat % CHUNK == 0 and n_lat % GRID_W == 0
    assert w_in.shape[2] == 4 * ret_width + 3 * conv_width and ret_width + conv_width == d

    s = jnp.zeros((ADA_ROWS, d), _f32).at[0].set(jax.nn.silu(c[0])).at[1].set(jax.nn.silu(c_ctx))
    mods = _ada_mod(s, ada_w_down, ada_w_up, ada_b)
    cos, sin = _rope_tables(n_lat, n_ctx, head_dim)
    lg = jnp.stack([jax.nn.log_sigmoid(ret_decay_fwd.astype(_f32)),
                    jax.nn.log_sigmoid(ret_decay_bwd.astype(_f32))], axis=1)

    h = None
    for l in range(depth):
        sh1, sc1, g1, sh2, sc2, g2 = [mods[l, :, j * d:(j + 1) * d] for j in range(N_MOD)]
        if l == 0:
            h, a = _norm_mod_first(x[0], ctx[0], norm1_w[l], sh1, sc1)
        else:
            a = _norm_mod(h, norm1_w[l], sh1, sc1, n_lat)
        qk, rest = _in_proj(a, w_in, l, cos, sin, ret_width, head_dim)
        ret = _retention(qk, rest, lg[l], ret_norm_w[l], n_ctx, heads, head_dim)
        conv = _gated_conv(rest, conv_w[l].T, n_lat, ret_width, conv_width)
        m_out = n_lat if l == depth - 1 else n_lat + n_ctx
        h = _out_proj(ret, conv, w_out, l, h, g1, n_lat, m_out)
        a = _norm_mod(h, norm2_w[l], sh2, sc2, n_lat)
        z = _mlp_up(a, mlp_w1, l)
        h = _mlp_down(z, mlp_w2, l, h, g2, n_lat)
    return _final_norm(h, final_norm_w, n_lat)[None]
```

```python
import functools

import jax
import jax.numpy as jnp
from jax import lax
from jax.experimental import pallas as pl
from jax.experimental.pallas import tpu as pltpu

GRID_W = 64
CHUNK = 128
ROPE_THETA = 10000.0
EPS = 1e-6
N_MOD = 6
ADA_ROWS = 8
NORM_GROUP = 16
NORM_LANES = 128
NORM_TILE = 512
NORM_BUNDLE = 8
CAST_ROWS = 256
RET_CHUNKS = (256, 128)
V7X_VMEM_LIMIT_BYTES = 60 * 1024 * 1024

_bf16 = jnp.bfloat16
_f32 = jnp.float32


def _block(total, target, align):
    best = None
    for d in range(align, min(total, target) + 1, align):
        if total % d == 0:
            best = d
    if best is None:
        raise ValueError(f"no block for {total} (target {target}, align {align})")
    return best


def _params(sem):
    return pltpu.CompilerParams(dimension_semantics=sem, vmem_limit_bytes=V7X_VMEM_LIMIT_BYTES)


def _fetch_weight_block(w_hbm, stage_ref, wb_ref, sem, layer, t, n_blocks, origin):
    rows, cols = stage_ref.shape

    def aligned(v, a):
        return v if isinstance(v, int) else pl.multiple_of(v, a)

    def copy(tt):
        r0, c0 = origin(tt)
        src = w_hbm.at[layer, pl.ds(aligned(r0, rows), rows), pl.ds(aligned(c0, cols), cols)]
        return pltpu.make_async_copy(src, stage_ref, sem.at[0])

    @pl.when(t == 0)
    def _():
        copy(t).start()

    copy(t).wait()

    def cast(r, carry):
        sl = pl.ds(pl.multiple_of(r * CAST_ROWS, CAST_ROWS), CAST_ROWS)
        wb_ref[sl, :] = stage_ref[sl, :].astype(_bf16)
        return carry

    lax.fori_loop(0, rows // CAST_ROWS, cast, 0)

    @pl.when(t + 1 < n_blocks)
    def _():
        copy(t + 1).start()


def _weight_scratch(bk, bn):
    assert bk % CAST_ROWS == 0
    return [pltpu.VMEM((bk, bn), _f32), pltpu.VMEM((bk, bn), _bf16), pltpu.SemaphoreType.DMA((1,))]


def _ada_down_kernel(s_ref, wd_ref, t_ref):
    t_ref[0] = jnp.dot(s_ref[...].astype(_bf16), wd_ref[0].astype(_bf16),
                       preferred_element_type=_f32)


def _ada_up_kernel(t_ref, wu_ref, b_ref, m_ref):
    m_ref[0] = jnp.dot(t_ref[0].astype(_bf16), wu_ref[0].astype(_bf16),
                       preferred_element_type=_f32) + b_ref[0]


def _ada_mod(s, w_down, w_up, b):
    depth, d, rank = w_down.shape
    n = w_up.shape[2]
    t = pl.pallas_call(
        _ada_down_kernel,
        out_shape=jax.ShapeDtypeStruct((depth, ADA_ROWS, rank), _f32),
        grid=(depth,),
        in_specs=[pl.BlockSpec((ADA_ROWS, d), lambda l: (0, 0)),
                  pl.BlockSpec((1, d, rank), lambda l: (l, 0, 0))],
        out_specs=pl.BlockSpec((1, ADA_ROWS, rank), lambda l: (l, 0, 0)),
        compiler_params=_params(("arbitrary",)),
        name="ada_down",
    )(s, w_down)
    bn = _block(n, 4096, 128)
    return pl.pallas_call(
        _ada_up_kernel,
        out_shape=jax.ShapeDtypeStruct((depth, ADA_ROWS, n), _f32),
        grid=(depth, n // bn),
        in_specs=[pl.BlockSpec((1, ADA_ROWS, rank), lambda l, j: (l, 0, 0)),
                  pl.BlockSpec((1, rank, bn), lambda l, j: (l, 0, j)),
                  pl.BlockSpec((1, 1, bn), lambda l, j: (l, 0, j))],
        out_specs=pl.BlockSpec((1, ADA_ROWS, bn), lambda l, j: (l, 0, j)),
        compiler_params=_params(("arbitrary", "arbitrary")),
        name="ada_up",
    )(t, w_up, b.reshape(depth, 1, n))


def _row_select(mod_ref, row0, rows, n_lat):
    ridx = row0 + lax.broadcasted_iota(jnp.int32, (rows, 1), 0)
    return jnp.where(ridx < n_lat, mod_ref[0:1, :], mod_ref[1:2, :])


def _norm_mod_kernel(x_ref, w_ref, sh_ref, sc_ref, o_ref, gain_ref, shift_ref, rstd_ref, *, n_lat):
    rows, d = x_ref.shape
    g_rows = NORM_GROUP
    w = w_ref[...]
    for r in range(2):
        gain_ref[r * g_rows:(r + 1) * g_rows, :] = jnp.broadcast_to(
            w * (1.0 + sc_ref[r:r + 1, :]), (g_rows, d))
        shift_ref[r * g_rows:(r + 1) * g_rows, :] = jnp.broadcast_to(sh_ref[r:r + 1, :], (g_rows, d))

    def group(g):
        return pl.ds(pl.multiple_of(g * g_rows, g_rows), g_rows)

    def stats(g, carry):
        x = x_ref[group(g), :]
        rstd = lax.rsqrt(jnp.mean(x * x, axis=-1, keepdims=True) + EPS)
        rstd_ref[group(g), :] = jnp.broadcast_to(rstd, (g_rows, NORM_LANES))
        return carry

    lax.fori_loop(0, rows // g_rows, stats, 0, unroll=8)

    steps = rows // (g_rows * NORM_BUNDLE)
    lat_steps = jnp.clip(n_lat - pl.program_id(0) * rows, 0, rows) // (g_rows * NORM_BUNDLE)
    reps = NORM_TILE // NORM_LANES
    for ct in range(d // NORM_TILE):
        cols = slice(ct * NORM_TILE, (ct + 1) * NORM_TILE)
        for mod_row0, lo, hi in ((0, 0, lat_steps), (g_rows, lat_steps, steps)):
            gain = gain_ref[mod_row0:mod_row0 + g_rows, cols]
            shift = shift_ref[mod_row0:mod_row0 + g_rows, cols]

            def apply(s, carry, gain=gain, shift=shift, cols=cols):
                for b in range(NORM_BUNDLE):
                    rows_g = group(s * NORM_BUNDLE + b)
                    r = rstd_ref[rows_g, :]
                    rr = jnp.concatenate([r] * reps, axis=1)
                    o_ref[rows_g, cols] = ((x_ref[rows_g, cols] * rr) * gain
                                           + shift).astype(o_ref.dtype)
                return carry

            lax.fori_loop(lo, hi, apply, 0)


def _norm_scratch(br, d):
    assert d % NORM_TILE == 0 and br % (NORM_GROUP * NORM_BUNDLE) == 0
    return [pltpu.VMEM((2 * NORM_GROUP, d), _f32), pltpu.VMEM((2 * NORM_GROUP, d), _f32),
            pltpu.VMEM((br, NORM_LANES), _f32)]


def _norm_mod_first_kernel(x_ref, ctx_ref, w_ref, sh_ref, sc_ref, h_ref, o_ref, gain_ref,
                           shift_ref, rstd_ref, *, n_lat):
    in_lat = pl.program_id(0) * h_ref.shape[0] < n_lat

    @pl.when(in_lat)
    def _():
        h_ref[...] = x_ref[...]

    @pl.when(jnp.logical_not(in_lat))
    def _():
        h_ref[...] = ctx_ref[...]

    _norm_mod_kernel(h_ref, w_ref, sh_ref, sc_ref, o_ref, gain_ref, shift_ref, rstd_ref,
                     n_lat=n_lat)


def _norm_mod_first(x, ctx, w, sh, sc):
    n_lat, d = x.shape
    n_ctx = ctx.shape[0]
    m = n_lat + n_ctx
    br = _block(n_ctx, 512, 8 * NORM_GROUP)
    assert n_lat % br == 0
    lat_blocks = n_lat // br
    row_spec = pl.BlockSpec((br, d), lambda i: (i, 0))
    vec_spec = pl.BlockSpec((ADA_ROWS, d), lambda i: (0, 0))
    return pl.pallas_call(
        functools.partial(_norm_mod_first_kernel, n_lat=n_lat),
        out_shape=(jax.ShapeDtypeStruct((m, d), _f32), jax.ShapeDtypeStruct((m, d), _bf16)),
        grid=(m // br,),
        in_specs=[pl.BlockSpec((br, d), lambda i: (jnp.minimum(i, lat_blocks - 1), 0)),
                  pl.BlockSpec((br, d), lambda i: (jnp.maximum(i - lat_blocks, 0), 0)),
                  pl.BlockSpec((1, d), lambda i: (0, 0)), vec_spec, vec_spec],
        out_specs=(row_spec, row_spec),
        scratch_shapes=_norm_scratch(br, d),
        compiler_params=_params(("arbitrary",)),
        name="norm_mod_first",
    )(x, ctx, w.reshape(1, d), sh, sc)


def _norm_mod(h, w, sh, sc, n_lat):
    m, d = h.shape
    br = _block(m, 512, 8 * NORM_GROUP)
    assert n_lat % NORM_GROUP == 0
    return pl.pallas_call(
        functools.partial(_norm_mod_kernel, n_lat=n_lat),
        out_shape=jax.ShapeDtypeStruct((m, d), _bf16),
        grid=(m // br,),
        in_specs=[pl.BlockSpec((br, d), lambda i: (i, 0)),
                  pl.BlockSpec((1, d), lambda i: (0, 0)),
                  pl.BlockSpec((ADA_ROWS, d), lambda i: (0, 0)),
                  pl.BlockSpec((ADA_ROWS, d), lambda i: (0, 0))],
        out_specs=pl.BlockSpec((br, d), lambda i: (i, 0)),
        scratch_shapes=_norm_scratch(br, d),
        compiler_params=_params(("arbitrary",)),
        name="norm_mod",
    )(h, w.reshape(1, d), sh, sc)


def _final_norm_kernel(x_ref, w_ref, o_ref):
    x = x_ref[...]
    o_ref[...] = x * lax.rsqrt(jnp.mean(x * x, axis=-1, keepdims=True) + EPS) * w_ref[...]


def _final_norm(h, w, n_lat):
    d = h.shape[1]
    br = _block(n_lat, 256, 8)
    return pl.pallas_call(
        _final_norm_kernel,
        out_shape=jax.ShapeDtypeStruct((n_lat, d), _f32),
        grid=(n_lat // br,),
        in_specs=[pl.BlockSpec((br, d), lambda i: (i, 0)),
                  pl.BlockSpec((1, d), lambda i: (0, 0))],
        out_specs=pl.BlockSpec((br, d), lambda i: (i, 0)),
        compiler_params=_params(("arbitrary",)),
        name="final_norm",
    )(h, w.reshape(1, d))


def _in_proj_rope_kernel(a_ref, w_hbm, cos_ref, sin_ref, o_ref, stage_ref, wb_ref, sem, *,
                         layer, n_q_blocks, head_dim):
    j = pl.program_id(0)
    bn = wb_ref.shape[1]

    @pl.when(pl.program_id(1) == 0)
    def _():
        _fetch_weight_block(w_hbm, stage_ref, wb_ref, sem, layer, j, pl.num_programs(0),
                            lambda t: (0, t * bn))

    y = jnp.dot(a_ref[...], wb_ref[...], preferred_element_type=_f32)
    half = head_dim // 2
    scale = jnp.where(j < n_q_blocks, head_dim ** -0.5, 1.0).astype(_f32)
    cos = cos_ref[...] * scale
    sin = sin_ref[...] * scale
    for hd in range(bn // head_dim):
        c0 = hd * head_dim
        t1 = y[:, c0:c0 + half]
        t2 = y[:, c0 + half:c0 + head_dim]
        o_ref[:, c0:c0 + half] = (t1 * cos - t2 * sin).astype(o_ref.dtype)
        o_ref[:, c0 + half:c0 + head_dim] = (t1 * sin + t2 * cos).astype(o_ref.dtype)


def _in_proj_plain_kernel(a_ref, w_hbm, o_ref, stage_ref, wb_ref, sem, *, layer, col0):
    bn = wb_ref.shape[1]

    @pl.when(pl.program_id(1) == 0)
    def _():
        _fetch_weight_block(w_hbm, stage_ref, wb_ref, sem, layer, pl.program_id(0),
                            pl.num_programs(0), lambda t: (0, col0 + t * bn))

    o_ref[...] = jnp.dot(a_ref[...], wb_ref[...], preferred_element_type=_f32).astype(o_ref.dtype)


def _in_proj(a, w, layer, cos, sin, ret_width, head_dim):
    m, k = a.shape
    n = w.shape[2]
    bm = _block(m, 1056, 16)
    bn = _block(ret_width, 1024, head_dim)
    n_rope = 2 * ret_width
    assert (n - n_rope) % bn == 0
    half = head_dim // 2
    a_spec = pl.BlockSpec((bm, k), lambda j, i: (i, 0))
    o_spec = pl.BlockSpec((bm, bn), lambda j, i: (i, j))
    qk = pl.pallas_call(
        functools.partial(_in_proj_rope_kernel, layer=layer, n_q_blocks=ret_width // bn,
                          head_dim=head_dim),
        out_shape=jax.ShapeDtypeStruct((m, n_rope), _bf16),
        grid=(n_rope // bn, m // bm),
        in_specs=[a_spec, pl.BlockSpec(memory_space=pl.ANY),
                  pl.BlockSpec((bm, half), lambda j, i: (i, 0)),
                  pl.BlockSpec((bm, half), lambda j, i: (i, 0))],
        out_specs=o_spec,
        scratch_shapes=_weight_scratch(k, bn),
        compiler_params=_params(("arbitrary", "arbitrary")),
        name="in_proj_qk",
    )(a, w, cos, sin)
    rest = pl.pallas_call(
        functools.partial(_in_proj_plain_kernel, layer=layer, col0=n_rope),
        out_shape=jax.ShapeDtypeStruct((m, n - n_rope), _bf16),
        grid=((n - n_rope) // bn, m // bm),
        in_specs=[a_spec, pl.BlockSpec(memory_space=pl.ANY)],
        out_specs=o_spec,
        scratch_shapes=_weight_scratch(k, bn),
        compiler_params=_params(("arbitrary", "arbitrary")),
        name="in_proj_rest",
    )(a, w)
    return qk, rest


def _retention_kernel(lg_ref, q_ref, k_ref, v_ref, g_ref, w_ref, o_ref, snap_ref, s_ref,
                      decay_ref, *, chunk, n_ctx_chunks, unroll):
    head = pl.program_id(0)
    n_chunks = q_ref.shape[0] // chunk
    n_lat_chunks = n_chunks - n_ctx_chunks
    lg_f = lg_ref[0, head]
    lg_b = lg_ref[1, head]

    ii = lax.broadcasted_iota(jnp.int32, (chunk, chunk), 0)
    jj = lax.broadcasted_iota(jnp.int32, (chunk, chunk), 1)
    diff = (ii - jj).astype(_f32)
    decay_ref[...] = jnp.where(diff >= 0.0, jnp.exp(lg_f * jnp.maximum(diff, 0.0)),
                               jnp.exp(lg_b * jnp.maximum(-diff, 0.0)))
    pos = lax.broadcasted_iota(jnp.int32, (chunk, 1), 0).astype(_f32)
    chunk_len = jnp.full((1, 1), float(chunk), _f32)
    q_decay_f = jnp.exp(lg_f * (pos + 1.0))
    k_decay_f = jnp.exp(lg_f * (chunk - 1.0 - pos))
    gamma_f = jnp.exp(lg_f * chunk_len)
    q_decay_b = jnp.exp(lg_b * (chunk - pos))
    k_decay_b = jnp.exp(lg_b * pos)
    gamma_b = jnp.exp(lg_b * chunk_len)

    def scaled(t, d):
        return (t.astype(_f32) * d).astype(_bf16)

    def kv_outer(kc, vc, d):
        return lax.dot_general(scaled(kc, d), vc, (((0,), (0,)), ((), ())),
                               preferred_element_type=_f32)

    def chunk_rows(c):
        return pl.ds(pl.multiple_of(c * chunk, chunk), chunk)

    s_ref[...] = jnp.zeros_like(s_ref)

    def fwd(t, carry):
        c = jnp.where(t < n_ctx_chunks, t + n_lat_chunks, t - n_ctx_chunks)
        rows = chunk_rows(c)
        snap_ref[c] = s_ref[...].astype(_bf16)
        s_ref[...] = gamma_f * s_ref[...] + kv_outer(k_ref[rows, :], v_ref[rows, :], k_decay_f)
        return carry

    lax.fori_loop(0, n_chunks, fwd, 0, unroll=unroll)

    s_ref[...] = jnp.zeros_like(s_ref)
    w = w_ref[...]

    def bwd(t, carry):
        c = n_chunks - 1 - t
        rows = chunk_rows(c)
        qc, kc, vc = q_ref[rows, :], k_ref[rows, :], v_ref[rows, :]
        s = lax.dot_general(qc, kc, (((1,), (1,)), ((), ())), preferred_element_type=_f32)
        o = jnp.dot((s * decay_ref[...]).astype(_bf16), vc, preferred_element_type=_f32)
        o += q_decay_f * jnp.dot(qc, snap_ref[c], preferred_element_type=_f32)
        o += q_decay_b * jnp.dot(qc, s_ref[...].astype(_bf16), preferred_element_type=_f32)
        s_ref[...] = gamma_b * s_ref[...] + kv_outer(kc, vc, k_decay_b)
        r = o * lax.rsqrt(jnp.mean(o * o, axis=-1, keepdims=True) + EPS) * w
        g = g_ref[rows, :].astype(_f32)
        o_ref[rows, :] = (g * (1.0 / (1.0 + jnp.exp(-g))) * r).astype(o_ref.dtype)
        return carry

    lax.fori_loop(0, n_chunks, bwd, 0, unroll=unroll)


def _retention(qk, rest, lg, norm_w, n_ctx, heads, head_dim):
    m = qk.shape[0]
    chunk = max(c for c in RET_CHUNKS if n_ctx % c == 0 and m % c == 0)
    n_chunks = m // chunk
    ret_width = heads * head_dim
    blk = lambda part: pl.BlockSpec((m, head_dim), lambda h: (0, part * heads + h))
    return pl.pallas_call(
        functools.partial(_retention_kernel, chunk=chunk, n_ctx_chunks=n_ctx // chunk,
                          unroll=max(u for u in (3, 2, 1) if n_chunks % u == 0)),
        out_shape=jax.ShapeDtypeStruct((m, ret_width), _bf16),
        grid=(heads,),
        in_specs=[pl.BlockSpec(memory_space=pltpu.SMEM),
                  blk(0), blk(1), blk(0), blk(1),
                  pl.BlockSpec((1, head_dim), lambda h: (0, h))],
        out_specs=pl.BlockSpec((m, head_dim), lambda h: (0, h)),
        scratch_shapes=[pltpu.VMEM((n_chunks, head_dim, head_dim), _bf16),
                        pltpu.VMEM((head_dim, head_dim), _f32),
                        pltpu.VMEM((chunk, chunk), _f32)],
        compiler_params=_params(("arbitrary",)),
        name="retention",
    )(lg, qk, qk, rest, rest, norm_w.reshape(1, ret_width))


def _conv_kernel(bg_ref, cg_ref, xc_ref, w_ref, o_ref, *, n_lat, rows_per_step):
    m = o_ref.shape[0]
    r = rows_per_step
    w0, w1, w2 = w_ref[0:1, :], w_ref[1:2, :], w_ref[2:3, :]
    ridx = lax.broadcasted_iota(jnp.int32, (r, 1), 0)

    def u_rows(start, size):
        rows = pl.ds(pl.multiple_of(start, 16), size)
        return cg_ref[rows, :].astype(_f32) * xc_ref[rows, :].astype(_f32)

    def step(i, prev_last):
        r0 = i * r
        u = u_rows(r0, r)
        nxt_first = u_rows(jnp.minimum(r0 + r, m - 16), 16)[0:1, :]
        prev_row = jnp.where((r0 == 0) | (r0 == n_lat), 0.0, prev_last)
        next_row = jnp.where((r0 + r == n_lat) | (r0 + r == m), 0.0, nxt_first)
        up = jnp.where(ridx == 0, prev_row, pltpu.roll(u, 1, axis=0))
        dn = jnp.where(ridx == r - 1, next_row, pltpu.roll(u, r - 1, axis=0))
        conv = up * w0 + u * w1 + dn * w2
        rows = pl.ds(pl.multiple_of(r0, 16), r)
        o_ref[rows, :] = (bg_ref[rows, :].astype(_f32) * conv).astype(o_ref.dtype)
        return u[r - 1:r, :]

    lax.fori_loop(0, m // r, step, jnp.zeros((1, o_ref.shape[1]), _f32))


def _gated_conv(y, conv_w_t, n_lat, ret_width, conv_width):
    m = y.shape[0]
    bc = _block(conv_width, 256, 128)
    nb = conv_width // bc
    base = 2 * ret_width // bc
    r = _block(m - n_lat, 128, 16)
    assert n_lat % r == 0
    blk = lambda part: pl.BlockSpec((m, bc), lambda j: (0, base + part * nb + j))
    return pl.pallas_call(
        functools.partial(_conv_kernel, n_lat=n_lat, rows_per_step=r),
        out_shape=jax.ShapeDtypeStruct((m, conv_width), _bf16),
        grid=(nb,),
        in_specs=[blk(0), blk(1), blk(2), pl.BlockSpec((3, bc), lambda j: (0, j))],
        out_specs=pl.BlockSpec((m, bc), lambda j: (0, j)),
        compiler_params=_params(("arbitrary",)),
        name="gated_conv",
    )(y, y, y, conv_w_t)


def _out_proj_kernel(r_ref, c_ref, w_hbm, h_ref, g_ref, o_ref, stage_ref, wb_ref, sem, *,
                     layer, n_lat):
    rows, kr = r_ref.shape
    bn = wb_ref.shape[1]

    @pl.when(pl.program_id(1) == 0)
    def _():
        _fetch_weight_block(w_hbm, stage_ref, wb_ref, sem, layer, pl.program_id(0),
                            pl.num_programs(0), lambda t: (0, t * bn))

    y = (jnp.dot(r_ref[...], wb_ref[0:kr, :], preferred_element_type=_f32)
         + jnp.dot(c_ref[...], wb_ref[kr:, :], preferred_element_type=_f32))
    gate = _row_select(g_ref, pl.program_id(1) * rows, rows, n_lat)
    o_ref[...] = h_ref[...] + gate * y


def _out_proj(ret, conv, w, layer, h, gate, n_lat, m):
    kr = ret.shape[1]
    kc = conv.shape[1]
    n = w.shape[2]
    bm = _block(m, 704, 16)
    bn = _block(n, 1024, 128)
    return pl.pallas_call(
        functools.partial(_out_proj_kernel, layer=layer, n_lat=n_lat),
        out_shape=jax.ShapeDtypeStruct((m, n), _f32),
        grid=(n // bn, m // bm),
        in_specs=[pl.BlockSpec((bm, kr), lambda j, i: (i, 0)),
                  pl.BlockSpec((bm, kc), lambda j, i: (i, 0)),
                  pl.BlockSpec(memory_space=pl.ANY),
                  pl.BlockSpec((bm, bn), lambda j, i: (i, j)),
                  pl.BlockSpec((ADA_ROWS, bn), lambda j, i: (0, j))],
        out_specs=pl.BlockSpec((bm, bn), lambda j, i: (i, j)),
        scratch_shapes=_weight_scratch(kr + kc, bn),
        compiler_params=_params(("arbitrary", "arbitrary")),
        name="out_proj",
    )(ret, conv, w, h, gate)


def _mlp_up_kernel(a_ref, w_hbm, o_ref, stage_ref, wb_ref, sem, *, layer):
    bn = wb_ref.shape[1]

    @pl.when(pl.program_id(1) == 0)
    def _():
        _fetch_weight_block(w_hbm, stage_ref, wb_ref, sem, layer, pl.program_id(0),
                            pl.num_programs(0), lambda t: (0, t * bn))

    z = jnp.maximum(jnp.dot(a_ref[...], wb_ref[...], preferred_element_type=_f32), 0.0)
    o_ref[...] = (z * z).astype(o_ref.dtype)


def _mlp_up(a, w, layer):
    m, k = a.shape
    n = w.shape[2]
    bm = _block(m, 1056, 16)
    bn = _block(n, 1024, 128)
    return pl.pallas_call(
        functools.partial(_mlp_up_kernel, layer=layer),
        out_shape=jax.ShapeDtypeStruct((m, n), _bf16),
        grid=(n // bn, m // bm),
        in_specs=[pl.BlockSpec((bm, k), lambda j, i: (i, 0)),
                  pl.BlockSpec(memory_space=pl.ANY)],
        out_specs=pl.BlockSpec((bm, bn), lambda j, i: (i, j)),
        scratch_shapes=_weight_scratch(k, bn),
        compiler_params=_params(("arbitrary", "arbitrary")),
        name="mlp_up",
    )(a, w)


def _mlp_down_kernel(z_ref, w_hbm, h_ref, g_ref, o_ref, stage_ref, wb_ref, sem, acc_ref, *,
                     layer, n_lat, nk):
    j, kk, i = pl.program_id(0), pl.program_id(1), pl.program_id(2)
    bm = z_ref.shape[0]
    bk, bn = wb_ref.shape

    @pl.when(i == 0)
    def _():
        _fetch_weight_block(w_hbm, stage_ref, wb_ref, sem, layer, j * nk + kk,
                            pl.num_programs(0) * nk, lambda t: ((t % nk) * bk, (t // nk) * bn))

    rows = pl.ds(pl.multiple_of(i * bm, bm), bm)

    def partial_product():
        return jnp.dot(z_ref[...], wb_ref[...], preferred_element_type=_f32)

    def finish(total):
        gate = _row_select(g_ref, i * bm, bm, n_lat)
        o_ref[...] = h_ref[...] + gate * total

    if nk == 1:
        finish(partial_product())
        return

    @pl.when(kk == 0)
    def _():
        acc_ref[rows, :] = partial_product()

    if nk > 2:
        @pl.when((kk > 0) & (kk < nk - 1))
        def _():
            acc_ref[rows, :] += partial_product()

    @pl.when(kk == nk - 1)
    def _():
        finish(acc_ref[rows, :] + partial_product())


def _mlp_down(z, w, layer, h, gate, n_lat):
    m, k = z.shape
    n = w.shape[2]
    bm = _block(m, 1056, 16)
    bn = _block(n, 512, 128)
    bk = _block(k, 4096, CAST_ROWS)
    nk = k // bk
    res_map = lambda j, kk, i: (jnp.where(kk == nk - 1, i, 0), j)
    return pl.pallas_call(
        functools.partial(_mlp_down_kernel, layer=layer, n_lat=n_lat, nk=nk),
        out_shape=jax.ShapeDtypeStruct((m, n), _f32),
        grid=(n // bn, nk, m // bm),
        in_specs=[pl.BlockSpec((bm, bk), lambda j, kk, i: (i, kk)),
                  pl.BlockSpec(memory_space=pl.ANY),
                  pl.BlockSpec((bm, bn), res_map),
                  pl.BlockSpec((ADA_ROWS, bn), lambda j, kk, i: (0, j))],
        out_specs=pl.BlockSpec((bm, bn), res_map),
        scratch_shapes=_weight_scratch(bk, bn) + [pltpu.VMEM((m, bn), _f32)],
        compiler_params=_params(("arbitrary", "arbitrary", "arbitrary")),
        name="mlp_down",
    )(z, w, h, gate)


def _rope_tables(n_lat, n_ctx, head_dim):
    rows = n_lat // GRID_W
    row = jnp.broadcast_to(jnp.arange(rows, dtype=_f32)[:, None], (rows, GRID_W)).reshape(-1)
    col = jnp.broadcast_to(jnp.arange(GRID_W, dtype=_f32)[None, :], (rows, GRID_W)).reshape(-1)
    nf = head_dim // 4
    inv = ROPE_THETA ** (-jnp.arange(nf, dtype=_f32) / nf)
    ang = jnp.concatenate([row[:, None] * inv, col[:, None] * inv], axis=-1)
    cos = jnp.concatenate([jnp.cos(ang), jnp.ones((n_ctx, 2 * nf), _f32)], axis=0)
    sin = jnp.concatenate([jnp.sin(ang), jnp.zeros((n_ctx, 2 * nf), _f32)], axis=0)
    return cos, sin


def kernel(x, c, ctx, c_ctx, ada_w_down, ada_w_up, ada_b, norm1_w, norm2_w, w_in, ret_decay_fwd,
           ret_decay_bwd, ret_norm_w, conv_w, w_out, mlp_w1, mlp_w2, final_norm_w):
    batch, n_lat, d = x.shape
    n_ctx = ctx.shape[1]
    depth = w_in.shape[0]
    heads = ret_decay_fwd.shape[1]
    ret_width = ret_norm_w.shape[1]
    conv_width = conv_w.shape[1]
    head_dim = ret_width // heads
    assert batch == 1 and c.shape[0] == 1
    assert n_ctx % CHUNK == 0 and n_lat % CHUNK == 0 and n_lat % GRID_W == 0
    assert w_in.shape[2] == 4 * ret_width + 3 * conv_width and ret_width + conv_width == d

    s = jnp.zeros((ADA_ROWS, d), _f32).at[0].set(jax.nn.silu(c[0])).at[1].set(jax.nn.silu(c_ctx))
    mods = _ada_mod(s, ada_w_down, ada_w_up, ada_b)
    cos, sin = _rope_tables(n_lat, n_ctx, head_dim)
    lg = jnp.stack([jax.nn.log_sigmoid(ret_decay_fwd.astype(_f32)),
                    jax.nn.log_sigmoid(ret_decay_bwd.astype(_f32))], axis=1)

    h = None
    for l in range(depth):
        sh1, sc1, g1, sh2, sc2, g2 = [mods[l, :, j * d:(j + 1) * d] for j in range(N_MOD)]
        if l == 0:
            h, a = _norm_mod_first(x[0], ctx[0], norm1_w[l], sh1, sc1)
        else:
            a = _norm_mod(h, norm1_w[l], sh1, sc1, n_lat)
        qk, rest = _in_proj(a, w_in, l, cos, sin, ret_width, head_dim)
        ret = _retention(qk, rest, lg[l], ret_norm_w[l], n_ctx, heads, head_dim)
        conv = _gated_conv(rest, conv_w[l].T, n_lat, ret_width, conv_width)
        m_out = n_lat if l == depth - 1 else n_lat + n_ctx
        h = _out_proj(ret, conv, w_out, l, h, g1, n_lat, m_out)
        a = _norm_mod(h, norm2_w[l], sh2, sc2, n_lat)
        z = _mlp_up(a, mlp_w1, l)
        h = _mlp_down(z, mlp_w2, l, h, g2, n_lat)
    return _final_norm(h, final_norm_w, n_lat)[None]
```

```python
import functools

import jax
import jax.numpy as jnp
from jax import lax
from jax.experimental import pallas as pl
from jax.experimental.pallas import tpu as pltpu

GRID_W = 64
CHUNK = 128
ROPE_THETA = 10000.0
EPS = 1e-6
N_MOD = 6
ADA_ROWS = 8
NORM_GROUP = 16
NORM_LANES = 128
NORM_TILE = 512
NORM_BUNDLE = 8
CAST_ROWS = 256
RET_CHUNKS = (256, 128)
MXU_COLS = 256
V7X_VMEM_LIMIT_BYTES = 60 * 1024 * 1024

_bf16 = jnp.bfloat16
_f32 = jnp.float32


def _block(total, target, align):
    best = None
    for d in range(align, min(total, target) + 1, align):
        if total % d == 0:
            best = d
    if best is None:
        raise ValueError(f"no block for {total} (target {target}, align {align})")
    return best


def _params(sem):
    return pltpu.CompilerParams(dimension_semantics=sem, vmem_limit_bytes=V7X_VMEM_LIMIT_BYTES)


def _fetch_weight_block(w_hbm, stage_ref, wb_ref, sem, layer, t, n_blocks, origin):
    rows, cols = stage_ref.shape

    def aligned(v, a):
        return v if isinstance(v, int) else pl.multiple_of(v, a)

    def copy(tt):
        r0, c0 = origin(tt)
        src = w_hbm.at[layer, pl.ds(aligned(r0, rows), rows), pl.ds(aligned(c0, cols), cols)]
        return pltpu.make_async_copy(src, stage_ref, sem.at[0])

    @pl.when(t == 0)
    def _():
        copy(t).start()

    copy(t).wait()

    def cast(r, carry):
        sl = pl.ds(pl.multiple_of(r * CAST_ROWS, CAST_ROWS), CAST_ROWS)
        wb_ref[sl, :] = stage_ref[sl, :].astype(_bf16)
        return carry

    lax.fori_loop(0, rows // CAST_ROWS, cast, 0)

    @pl.when(t + 1 < n_blocks)
    def _():
        copy(t + 1).start()


def _column_slices(bn):
    step = MXU_COLS if bn % MXU_COLS == 0 else bn
    return [slice(c0, c0 + step) for c0 in range(0, bn, step)]


def _weight_scratch(bk, bn):
    assert bk % CAST_ROWS == 0
    return [pltpu.VMEM((bk, bn), _f32), pltpu.VMEM((bk, bn), _bf16), pltpu.SemaphoreType.DMA((1,))]


def _ada_down_kernel(s_ref, wd_ref, t_ref):
    t_ref[0] = jnp.dot(s_ref[...].astype(_bf16), wd_ref[0].astype(_bf16),
                       preferred_element_type=_f32)


def _ada_up_kernel(t_ref, wu_ref, b_ref, m_ref):
    m_ref[0] = jnp.dot(t_ref[0].astype(_bf16), wu_ref[0].astype(_bf16),
                       preferred_element_type=_f32) + b_ref[0]


def _ada_mod(s, w_down, w_up, b):
    depth, d, rank = w_down.shape
    n = w_up.shape[2]
    t = pl.pallas_call(
        _ada_down_kernel,
        out_shape=jax.ShapeDtypeStruct((depth, ADA_ROWS, rank), _f32),
        grid=(depth,),
        in_specs=[pl.BlockSpec((ADA_ROWS, d), lambda l: (0, 0)),
                  pl.BlockSpec((1, d, rank), lambda l: (l, 0, 0))],
        out_specs=pl.BlockSpec((1, ADA_ROWS, rank), lambda l: (l, 0, 0)),
        compiler_params=_params(("arbitrary",)),
        name="ada_down",
    )(s, w_down)
    bn = _block(n, 4096, 128)
    return pl.pallas_call(
        _ada_up_kernel,
        out_shape=jax.ShapeDtypeStruct((depth, ADA_ROWS, n), _f32),
        grid=(depth, n // bn),
        in_specs=[pl.BlockSpec((1, ADA_ROWS, rank), lambda l, j: (l, 0, 0)),
                  pl.BlockSpec((1, rank, bn), lambda l, j: (l, 0, j)),
                  pl.BlockSpec((1, 1, bn), lambda l, j: (l, 0, j))],
        out_specs=pl.BlockSpec((1, ADA_ROWS, bn), lambda l, j: (l, 0, j)),
        compiler_params=_params(("arbitrary", "arbitrary")),
        name="ada_up",
    )(t, w_up, b.reshape(depth, 1, n))


def _row_select(mod_ref, row0, rows, n_lat):
    ridx = row0 + lax.broadcasted_iota(jnp.int32, (rows, 1), 0)
    return jnp.where(ridx < n_lat, mod_ref[0:1, :], mod_ref[1:2, :])


def _norm_mod_kernel(x_ref, w_ref, sh_ref, sc_ref, o_ref, gain_ref, shift_ref, rstd_ref, *, n_lat):
    rows, d = x_ref.shape
    g_rows = NORM_GROUP
    w = w_ref[...]
    for r in range(2):
        gain_ref[r * g_rows:(r + 1) * g_rows, :] = jnp.broadcast_to(
            w * (1.0 + sc_ref[r:r + 1, :]), (g_rows, d))
        shift_ref[r * g_rows:(r + 1) * g_rows, :] = jnp.broadcast_to(sh_ref[r:r + 1, :], (g_rows, d))

    def group(g):
        return pl.ds(pl.multiple_of(g * g_rows, g_rows), g_rows)

    def stats(g, carry):
        x = x_ref[group(g), :]
        rstd = lax.rsqrt(jnp.mean(x * x, axis=-1, keepdims=True) + EPS)
        rstd_ref[group(g), :] = jnp.broadcast_to(rstd, (g_rows, NORM_LANES))
        return carry

    lax.fori_loop(0, rows // g_rows, stats, 0, unroll=8)

    steps = rows // (g_rows * NORM_BUNDLE)
    lat_steps = jnp.clip(n_lat - pl.program_id(0) * rows, 0, rows) // (g_rows * NORM_BUNDLE)
    reps = NORM_TILE // NORM_LANES
    for ct in range(d // NORM_TILE):
        cols = slice(ct * NORM_TILE, (ct + 1) * NORM_TILE)
        for mod_row0, lo, hi in ((0, 0, lat_steps), (g_rows, lat_steps, steps)):
            gain = gain_ref[mod_row0:mod_row0 + g_rows, cols]
            shift = shift_ref[mod_row0:mod_row0 + g_rows, cols]

            def apply(s, carry, gain=gain, shift=shift, cols=cols):
                for b in range(NORM_BUNDLE):
                    rows_g = group(s * NORM_BUNDLE + b)
                    r = rstd_ref[rows_g, :]
                    rr = jnp.concatenate([r] * reps, axis=1)
                    o_ref[rows_g, cols] = ((x_ref[rows_g, cols] * rr) * gain
                                           + shift).astype(o_ref.dtype)
                return carry

            lax.fori_loop(lo, hi, apply, 0)


def _norm_scratch(br, d):
    assert d % NORM_TILE == 0 and br % (NORM_GROUP * NORM_BUNDLE) == 0
    return [pltpu.VMEM((2 * NORM_GROUP, d), _f32), pltpu.VMEM((2 * NORM_GROUP, d), _f32),
            pltpu.VMEM((br, NORM_LANES), _f32)]


def _norm_mod_first_kernel(x_ref, ctx_ref, w_ref, sh_ref, sc_ref, h_ref, o_ref, gain_ref,
                           shift_ref, rstd_ref, *, n_lat):
    in_lat = pl.program_id(0) * h_ref.shape[0] < n_lat

    @pl.when(in_lat)
    def _():
        h_ref[...] = x_ref[...]

    @pl.when(jnp.logical_not(in_lat))
    def _():
        h_ref[...] = ctx_ref[...]

    _norm_mod_kernel(h_ref, w_ref, sh_ref, sc_ref, o_ref, gain_ref, shift_ref, rstd_ref,
                     n_lat=n_lat)


def _norm_mod_first(x, ctx, w, sh, sc):
    n_lat, d = x.shape
    n_ctx = ctx.shape[0]
    m = n_lat + n_ctx
    br = _block(n_ctx, 512, 8 * NORM_GROUP)
    assert n_lat % br == 0
    lat_blocks = n_lat // br
    row_spec = pl.BlockSpec((br, d), lambda i: (i, 0))
    vec_spec = pl.BlockSpec((ADA_ROWS, d), lambda i: (0, 0))
    return pl.pallas_call(
        functools.partial(_norm_mod_first_kernel, n_lat=n_lat),
        out_shape=(jax.ShapeDtypeStruct((m, d), _f32), jax.ShapeDtypeStruct((m, d), _bf16)),
        grid=(m // br,),
        in_specs=[pl.BlockSpec((br, d), lambda i: (jnp.minimum(i, lat_blocks - 1), 0)),
                  pl.BlockSpec((br, d), lambda i: (jnp.maximum(i - lat_blocks, 0), 0)),
                  pl.BlockSpec((1, d), lambda i: (0, 0)), vec_spec, vec_spec],
        out_specs=(row_spec, row_spec),
        scratch_shapes=_norm_scratch(br, d),
        compiler_params=_params(("arbitrary",)),
        name="norm_mod_first",
    )(x, ctx, w.reshape(1, d), sh, sc)


def _norm_mod(h, w, sh, sc, n_lat):
    m, d = h.shape
    br = _block(m, 512, 8 * NORM_GROUP)
    assert n_lat % NORM_GROUP == 0
    return pl.pallas_call(
        functools.partial(_norm_mod_kernel, n_lat=n_lat),
        out_shape=jax.ShapeDtypeStruct((m, d), _bf16),
        grid=(m // br,),
        in_specs=[pl.BlockSpec((br, d), lambda i: (i, 0)),
                  pl.BlockSpec((1, d), lambda i: (0, 0)),
                  pl.BlockSpec((ADA_ROWS, d), lambda i: (0, 0)),
                  pl.BlockSpec((ADA_ROWS, d), lambda i: (0, 0))],
        out_specs=pl.BlockSpec((br, d), lambda i: (i, 0)),
        scratch_shapes=_norm_scratch(br, d),
        compiler_params=_params(("arbitrary",)),
        name="norm_mod",
    )(h, w.reshape(1, d), sh, sc)


def _final_norm_kernel(x_ref, w_ref, o_ref):
    x = x_ref[...]
    o_ref[...] = x * lax.rsqrt(jnp.mean(x * x, axis=-1, keepdims=True) + EPS) * w_ref[...]


def _final_norm(h, w, n_lat):
    d = h.shape[1]
    br = _block(n_lat, 256, 8)
    return pl.pallas_call(
        _final_norm_kernel,
        out_shape=jax.ShapeDtypeStruct((n_lat, d), _f32),
        grid=(n_lat // br,),
        in_specs=[pl.BlockSpec((br, d), lambda i: (i, 0)),
                  pl.BlockSpec((1, d), lambda i: (0, 0))],
        out_specs=pl.BlockSpec((br, d), lambda i: (i, 0)),
        compiler_params=_params(("arbitrary",)),
        name="final_norm",
    )(h, w.reshape(1, d))


def _in_proj_rope_kernel(a_ref, w_hbm, cos_ref, sin_ref, o_ref, stage_ref, wb_ref, sem, *,
                         layer, n_q_blocks, head_dim):
    j = pl.program_id(0)
    bn = wb_ref.shape[1]

    @pl.when(pl.program_id(1) == 0)
    def _():
        _fetch_weight_block(w_hbm, stage_ref, wb_ref, sem, layer, j, pl.num_programs(0),
                            lambda t: (0, t * bn))

    half = head_dim // 2
    scale = jnp.where(j < n_q_blocks, head_dim ** -0.5, 1.0).astype(_f32)
    cos = cos_ref[...] * scale
    sin = sin_ref[...] * scale
    for c0 in range(0, bn, head_dim):
        y = jnp.dot(a_ref[...], wb_ref[:, c0:c0 + head_dim], preferred_element_type=_f32)
        t1 = y[:, :half]
        t2 = y[:, half:]
        o_ref[:, c0:c0 + half] = (t1 * cos - t2 * sin).astype(o_ref.dtype)
        o_ref[:, c0 + half:c0 + head_dim] = (t1 * sin + t2 * cos).astype(o_ref.dtype)


def _in_proj_plain_kernel(a_ref, w_hbm, o_ref, stage_ref, wb_ref, sem, *, layer, col0):
    bn = wb_ref.shape[1]

    @pl.when(pl.program_id(1) == 0)
    def _():
        _fetch_weight_block(w_hbm, stage_ref, wb_ref, sem, layer, pl.program_id(0),
                            pl.num_programs(0), lambda t: (0, col0 + t * bn))

    for cols in _column_slices(bn):
        o_ref[:, cols] = jnp.dot(a_ref[...], wb_ref[:, cols],
                                 preferred_element_type=_f32).astype(o_ref.dtype)


def _in_proj(a, w, layer, cos, sin, ret_width, head_dim):
    m, k = a.shape
    n = w.shape[2]
    bm = _block(m, 1056, 16)
    bn = _block(ret_width, 1024, head_dim)
    n_rope = 2 * ret_width
    assert (n - n_rope) % bn == 0
    half = head_dim // 2
    a_spec = pl.BlockSpec((bm, k), lambda j, i: (i, 0))
    o_spec = pl.BlockSpec((bm, bn), lambda j, i: (i, j))
    qk = pl.pallas_call(
        functools.partial(_in_proj_rope_kernel, layer=layer, n_q_blocks=ret_width // bn,
                          head_dim=head_dim),
        out_shape=jax.ShapeDtypeStruct((m, n_rope), _bf16),
        grid=(n_rope // bn, m // bm),
        in_specs=[a_spec, pl.BlockSpec(memory_space=pl.ANY),
                  pl.BlockSpec((bm, half), lambda j, i: (i, 0)),
                  pl.BlockSpec((bm, half), lambda j, i: (i, 0))],
        out_specs=o_spec,
        scratch_shapes=_weight_scratch(k, bn),
        compiler_params=_params(("arbitrary", "arbitrary")),
        name="in_proj_qk",
    )(a, w, cos, sin)
    rest = pl.pallas_call(
        functools.partial(_in_proj_plain_kernel, layer=layer, col0=n_rope),
        out_shape=jax.ShapeDtypeStruct((m, n - n_rope), _bf16),
        grid=((n - n_rope) // bn, m // bm),
        in_specs=[a_spec, pl.BlockSpec(memory_space=pl.ANY)],
        out_specs=o_spec,
        scratch_shapes=_weight_scratch(k, bn),
        compiler_params=_params(("arbitrary", "arbitrary")),
        name="in_proj_rest",
    )(a, w)
    return qk, rest


def _retention_kernel(lg_ref, q_ref, k_ref, v_ref, g_ref, w_ref, o_ref, snap_ref, s_ref,
                      decay_ref, *, chunk, n_ctx_chunks, unroll):
    head = pl.program_id(0)
    n_chunks = q_ref.shape[0] // chunk
    n_lat_chunks = n_chunks - n_ctx_chunks
    lg_f = lg_ref[0, head]
    lg_b = lg_ref[1, head]

    ii = lax.broadcasted_iota(jnp.int32, (chunk, chunk), 0)
    jj = lax.broadcasted_iota(jnp.int32, (chunk, chunk), 1)
    diff = (ii - jj).astype(_f32)
    decay_ref[...] = jnp.where(diff >= 0.0, jnp.exp(lg_f * jnp.maximum(diff, 0.0)),
                               jnp.exp(lg_b * jnp.maximum(-diff, 0.0)))
    pos = lax.broadcasted_iota(jnp.int32, (chunk, 1), 0).astype(_f32)
    chunk_len = jnp.full((1, 1), float(chunk), _f32)
    q_decay_f = jnp.exp(lg_f * (pos + 1.0))
    k_decay_f = jnp.exp(lg_f * (chunk - 1.0 - pos))
    gamma_f = jnp.exp(lg_f * chunk_len)
    q_decay_b = jnp.exp(lg_b * (chunk - pos))
    k_decay_b = jnp.exp(lg_b * pos)
    gamma_b = jnp.exp(lg_b * chunk_len)

    def scaled(t, d):
        return (t.astype(_f32) * d).astype(_bf16)

    def kv_outer(kc, vc, d):
        return lax.dot_general(scaled(kc, d), vc, (((0,), (0,)), ((), ())),
                               preferred_element_type=_f32)

    def chunk_rows(c):
        return pl.ds(pl.multiple_of(c * chunk, chunk), chunk)

    s_ref[...] = jnp.zeros_like(s_ref)

    def fwd(t, carry):
        c = jnp.where(t < n_ctx_chunks, t + n_lat_chunks, t - n_ctx_chunks)
        rows = chunk_rows(c)
        snap_ref[c] = s_ref[...].astype(_bf16)
        s_ref[...] = gamma_f * s_ref[...] + kv_outer(k_ref[rows, :], v_ref[rows, :], k_decay_f)
        return carry

    lax.fori_loop(0, n_chunks, fwd, 0, unroll=unroll)

    s_ref[...] = jnp.zeros_like(s_ref)
    w = w_ref[...]

    def bwd(t, carry):
        c = n_chunks - 1 - t
        rows = chunk_rows(c)
        qc, kc, vc = q_ref[rows, :], k_ref[rows, :], v_ref[rows, :]
        s = lax.dot_general(qc, kc, (((1,), (1,)), ((), ())), preferred_element_type=_f32)
        o = jnp.dot((s * decay_ref[...]).astype(_bf16), vc, preferred_element_type=_f32)
        o += q_decay_f * jnp.dot(qc, snap_ref[c], preferred_element_type=_f32)
        o += q_decay_b * jnp.dot(qc, s_ref[...].astype(_bf16), preferred_element_type=_f32)
        s_ref[...] = gamma_b * s_ref[...] + kv_outer(kc, vc, k_decay_b)
        r = o * lax.rsqrt(jnp.mean(o * o, axis=-1, keepdims=True) + EPS) * w
        g = g_ref[rows, :].astype(_f32)
        o_ref[rows, :] = (g * (1.0 / (1.0 + jnp.exp(-g))) * r).astype(o_ref.dtype)
        return carry

    lax.fori_loop(0, n_chunks, bwd, 0, unroll=unroll)


def _retention(qk, rest, lg, norm_w, n_ctx, heads, head_dim):
    m = qk.shape[0]
    chunk = max(c for c in RET_CHUNKS if n_ctx % c == 0 and m % c == 0)
    n_chunks = m // chunk
    ret_width = heads * head_dim
    blk = lambda part: pl.BlockSpec((m, head_dim), lambda h: (0, part * heads + h))
    return pl.pallas_call(
        functools.partial(_retention_kernel, chunk=chunk, n_ctx_chunks=n_ctx // chunk,
                          unroll=max(u for u in (3, 2, 1) if n_chunks % u == 0)),
        out_shape=jax.ShapeDtypeStruct((m, ret_width), _bf16),
        grid=(heads,),
        in_specs=[pl.BlockSpec(memory_space=pltpu.SMEM),
                  blk(0), blk(1), blk(0), blk(1),
                  pl.BlockSpec((1, head_dim), lambda h: (0, h))],
        out_specs=pl.BlockSpec((m, head_dim), lambda h: (0, h)),
        scratch_shapes=[pltpu.VMEM((n_chunks, head_dim, head_dim), _bf16),
                        pltpu.VMEM((head_dim, head_dim), _f32),
                        pltpu.VMEM((chunk, chunk), _f32)],
        compiler_params=_params(("arbitrary",)),
        name="retention",
    )(lg, qk, qk, rest, rest, norm_w.reshape(1, ret_width))


def _conv_kernel(bg_ref, cg_ref, xc_ref, w_ref, o_ref, *, n_lat, rows_per_step):
    m = o_ref.shape[0]
    r = rows_per_step
    w0, w1, w2 = w_ref[0:1, :], w_ref[1:2, :], w_ref[2:3, :]
    ridx = lax.broadcasted_iota(jnp.int32, (r, 1), 0)

    def u_rows(start, size):
        rows = pl.ds(pl.multiple_of(start, 16), size)
        return cg_ref[rows, :].astype(_f32) * xc_ref[rows, :].astype(_f32)

    def step(i, prev_last):
        r0 = i * r
        u = u_rows(r0, r)
        nxt_first = u_rows(jnp.minimum(r0 + r, m - 16), 16)[0:1, :]
        prev_row = jnp.where((r0 == 0) | (r0 == n_lat), 0.0, prev_last)
        next_row = jnp.where((r0 + r == n_lat) | (r0 + r == m), 0.0, nxt_first)
        up = jnp.where(ridx == 0, prev_row, pltpu.roll(u, 1, axis=0))
        dn = jnp.where(ridx == r - 1, next_row, pltpu.roll(u, r - 1, axis=0))
        conv = up * w0 + u * w1 + dn * w2
        rows = pl.ds(pl.multiple_of(r0, 16), r)
        o_ref[rows, :] = (bg_ref[rows, :].astype(_f32) * conv).astype(o_ref.dtype)
        return u[r - 1:r, :]

    lax.fori_loop(0, m // r, step, jnp.zeros((1, o_ref.shape[1]), _f32))


def _gated_conv(y, conv_w_t, n_lat, ret_width, conv_width):
    m = y.shape[0]
    bc = _block(conv_width, 256, 128)
    nb = conv_width // bc
    base = 2 * ret_width // bc
    r = _block(m - n_lat, 128, 16)
    assert n_lat % r == 0
    blk = lambda part: pl.BlockSpec((m, bc), lambda j: (0, base + part * nb + j))
    return pl.pallas_call(
        functools.partial(_conv_kernel, n_lat=n_lat, rows_per_step=r),
        out_shape=jax.ShapeDtypeStruct((m, conv_width), _bf16),
        grid=(nb,),
        in_specs=[blk(0), blk(1), blk(2), pl.BlockSpec((3, bc), lambda j: (0, j))],
        out_specs=pl.BlockSpec((m, bc), lambda j: (0, j)),
        compiler_params=_params(("arbitrary",)),
        name="gated_conv",
    )(y, y, y, conv_w_t)


def _out_proj_kernel(r_ref, c_ref, w_hbm, h_ref, g_ref, o_ref, stage_ref, wb_ref, sem, *,
                     layer, n_lat):
    rows, kr = r_ref.shape
    bn = wb_ref.shape[1]

    @pl.when(pl.program_id(1) == 0)
    def _():
        _fetch_weight_block(w_hbm, stage_ref, wb_ref, sem, layer, pl.program_id(0),
                            pl.num_programs(0), lambda t: (0, t * bn))

    gate = _row_select(g_ref, pl.program_id(1) * rows, rows, n_lat)
    for cols in _column_slices(bn):
        y = (jnp.dot(r_ref[...], wb_ref[0:kr, cols], preferred_element_type=_f32)
             + jnp.dot(c_ref[...], wb_ref[kr:, cols], preferred_element_type=_f32))
        o_ref[:, cols] = h_ref[:, cols] + gate[:, cols] * y


def _out_proj(ret, conv, w, layer, h, gate, n_lat, m):
    kr = ret.shape[1]
    kc = conv.shape[1]
    n = w.shape[2]
    bm = _block(m, 704, 16)
    bn = _block(n, 1024, 128)
    return pl.pallas_call(
        functools.partial(_out_proj_kernel, layer=layer, n_lat=n_lat),
        out_shape=jax.ShapeDtypeStruct((m, n), _f32),
        grid=(n // bn, m // bm),
        in_specs=[pl.BlockSpec((bm, kr), lambda j, i: (i, 0)),
                  pl.BlockSpec((bm, kc), lambda j, i: (i, 0)),
                  pl.BlockSpec(memory_space=pl.ANY),
                  pl.BlockSpec((bm, bn), lambda j, i: (i, j)),
                  pl.BlockSpec((ADA_ROWS, bn), lambda j, i: (0, j))],
        out_specs=pl.BlockSpec((bm, bn), lambda j, i: (i, j)),
        scratch_shapes=_weight_scratch(kr + kc, bn),
        compiler_params=_params(("arbitrary", "arbitrary")),
        name="out_proj",
    )(ret, conv, w, h, gate)


def _mlp_up_kernel(a_ref, w_hbm, o_ref, stage_ref, wb_ref, sem, *, layer):
    bn = wb_ref.shape[1]

    @pl.when(pl.program_id(1) == 0)
    def _():
        _fetch_weight_block(w_hbm, stage_ref, wb_ref, sem, layer, pl.program_id(0),
                            pl.num_programs(0), lambda t: (0, t * bn))

    for cols in _column_slices(bn):
        z = jnp.maximum(jnp.dot(a_ref[...], wb_ref[:, cols], preferred_element_type=_f32), 0.0)
        o_ref[:, cols] = (z * z).astype(o_ref.dtype)


def _mlp_up(a, w, layer):
    m, k = a.shape
    n = w.shape[2]
    bm = _block(m, 1408, 16)
    bn = _block(n, 1024, 128)
    return pl.pallas_call(
        functools.partial(_mlp_up_kernel, layer=layer),
        out_shape=jax.ShapeDtypeStruct((m, n), _bf16),
        grid=(n // bn, m // bm),
        in_specs=[pl.BlockSpec((bm, k), lambda j, i: (i, 0)),
                  pl.BlockSpec(memory_space=pl.ANY)],
        out_specs=pl.BlockSpec((bm, bn), lambda j, i: (i, j)),
        scratch_shapes=_weight_scratch(k, bn),
        compiler_params=_params(("arbitrary", "arbitrary")),
        name="mlp_up",
    )(a, w)


def _mlp_down_kernel(z_ref, w_hbm, h_ref, g_ref, o_ref, stage_ref, wb_ref, sem, acc_ref, *,
                     layer, n_lat, nk):
    j, kk, i = pl.program_id(0), pl.program_id(1), pl.program_id(2)
    bm = z_ref.shape[0]
    bk, bn = wb_ref.shape

    @pl.when(i == 0)
    def _():
        _fetch_weight_block(w_hbm, stage_ref, wb_ref, sem, layer, j * nk + kk,
                            pl.num_programs(0) * nk, lambda t: ((t % nk) * bk, (t // nk) * bn))

    rows = pl.ds(pl.multiple_of(i * bm, bm), bm)

    def partial_product():
        return jnp.dot(z_ref[...], wb_ref[...], preferred_element_type=_f32)

    def finish(total):
        gate = _row_select(g_ref, i * bm, bm, n_lat)
        o_ref[...] = h_ref[...] + gate * total

    if nk == 1:
        finish(partial_product())
        return

    @pl.when(kk == 0)
    def _():
        acc_ref[rows, :] = partial_product()

    if nk > 2:
        @pl.when((kk > 0) & (kk < nk - 1))
        def _():
            acc_ref[rows, :] += partial_product()

    @pl.when(kk == nk - 1)
    def _():
        finish(acc_ref[rows, :] + partial_product())


def _mlp_down(z, w, layer, h, gate, n_lat):
    m, k = z.shape
    n = w.shape[2]
    bm = _block(m, 1056, 16)
    bn = _block(n, 512, 128)
    bk = _block(k, 4096, CAST_ROWS)
    nk = k // bk
    res_map = lambda j, kk, i: (jnp.where(kk == nk - 1, i, 0), j)
    return pl.pallas_call(
        functools.partial(_mlp_down_kernel, layer=layer, n_lat=n_lat, nk=nk),
        out_shape=jax.ShapeDtypeStruct((m, n), _f32),
        grid=(n // bn, nk, m // bm),
        in_specs=[pl.BlockSpec((bm, bk), lambda j, kk, i: (i, kk)),
                  pl.BlockSpec(memory_space=pl.ANY),
                  pl.BlockSpec((bm, bn), res_map),
                  pl.BlockSpec((ADA_ROWS, bn), lambda j, kk, i: (0, j))],
        out_specs=pl.BlockSpec((bm, bn), res_map),
        scratch_shapes=_weight_scratch(bk, bn) + [pltpu.VMEM((m, bn), _f32)],
        compiler_params=_params(("arbitrary", "arbitrary", "arbitrary")),
        name="mlp_down",
    )(z, w, h, gate)


def _rope_tables(n_lat, n_ctx, head_dim):
    rows = n_lat // GRID_W
    row = jnp.broadcast_to(jnp.arange(rows, dtype=_f32)[:, None], (rows, GRID_W)).reshape(-1)
    col = jnp.broadcast_to(jnp.arange(GRID_W, dtype=_f32)[None, :], (rows, GRID_W)).reshape(-1)
    nf = head_dim // 4
    inv = ROPE_THETA ** (-jnp.arange(nf, dtype=_f32) / nf)
    ang = jnp.concatenate([row[:, None] * inv, col[:, None] * inv], axis=-1)
    cos = jnp.concatenate([jnp.cos(ang), jnp.ones((n_ctx, 2 * nf), _f32)], axis=0)
    sin = jnp.concatenate([jnp.sin(ang), jnp.zeros((n_ctx, 2 * nf), _f32)], axis=0)
    return cos, sin


def kernel(x, c, ctx, c_ctx, ada_w_down, ada_w_up, ada_b, norm1_w, norm2_w, w_in, ret_decay_fwd,
           ret_decay_bwd, ret_norm_w, conv_w, w_out, mlp_w1, mlp_w2, final_norm_w):
    batch, n_lat, d = x.shape
    n_ctx = ctx.shape[1]
    depth = w_in.shape[0]
    heads = ret_decay_fwd.shape[1]
    ret_width = ret_norm_w.shape[1]
    conv_width = conv_w.shape[1]
    head_dim = ret_width // heads
    assert batch == 1 and c.shape[0] == 1
    assert n_ctx % CHUNK == 0 and n_lat % CHUNK == 0 and n_lat % GRID_W == 0
    assert w_in.shape[2] == 4 * ret_width + 3 * conv_width and ret_width + conv_width == d

    s = jnp.zeros((ADA_ROWS, d), _f32).at[0].set(jax.nn.silu(c[0])).at[1].set(jax.nn.silu(c_ctx))
    mods = _ada_mod(s, ada_w_down, ada_w_up, ada_b)
    cos, sin = _rope_tables(n_lat, n_ctx, head_dim)
    lg = jnp.stack([jax.nn.log_sigmoid(ret_decay_fwd.astype(_f32)),
                    jax.nn.log_sigmoid(ret_decay_bwd.astype(_f32))], axis=1)

    h = None
    for l in range(depth):
        sh1, sc1, g1, sh2, sc2, g2 = [mods[l, :, j * d:(j + 1) * d] for j in range(N_MOD)]
        if l == 0:
            h, a = _norm_mod_first(x[0], ctx[0], norm1_w[l], sh1, sc1)
        else:
            a = _norm_mod(h, norm1_w[l], sh1, sc1, n_lat)
        qk, rest = _in_proj(a, w_in, l, cos, sin, ret_width, head_dim)
        ret = _retention(qk, rest, lg[l], ret_norm_w[l], n_ctx, heads, head_dim)
        conv = _gated_conv(rest, conv_w[l].T, n_lat, ret_width, conv_width)
        m_out = n_lat if l == depth - 1 else n_lat + n_ctx
        h = _out_proj(ret, conv, w_out, l, h, g1, n_lat, m_out)
        a = _norm_mod(h, norm2_w[l], sh2, sc2, n_lat)
        z = _mlp_up(a, mlp_w1, l)
        h = _mlp_down(z, mlp_w2, l, h, g2, n_lat)
    return _final_norm(h, final_norm_w, n_lat)[None]
```

```python
import functools

import jax
import jax.numpy as jnp
from jax import lax
from jax.experimental import pallas as pl
from jax.experimental.pallas import tpu as pltpu

GRID_W = 64
CHUNK = 128
ROPE_THETA = 10000.0
EPS = 1e-6
N_MOD = 6
ADA_ROWS = 8
NORM_GROUP = 16
NORM_LANES = 128
NORM_TILE = 512
NORM_BUNDLE = 8
CAST_ROWS = 256
RET_CHUNKS = (256, 128)
MXU_COLS = 256
V7X_VMEM_LIMIT_BYTES = 60 * 1024 * 1024

_bf16 = jnp.bfloat16
_f32 = jnp.float32


def _block(total, target, align):
    best = None
    for d in range(align, min(total, target) + 1, align):
        if total % d == 0:
            best = d
    if best is None:
        raise ValueError(f"no block for {total} (target {target}, align {align})")
    return best


def _params(sem):
    return pltpu.CompilerParams(dimension_semantics=sem, vmem_limit_bytes=V7X_VMEM_LIMIT_BYTES)


def _fetch_weight_block(w_hbm, stage_ref, wb_ref, sem, layer, t, n_blocks, origin):
    rows, cols = stage_ref.shape

    def aligned(v, a):
        return v if isinstance(v, int) else pl.multiple_of(v, a)

    def copy(tt):
        r0, c0 = origin(tt)
        src = w_hbm.at[layer, pl.ds(aligned(r0, rows), rows), pl.ds(aligned(c0, cols), cols)]
        return pltpu.make_async_copy(src, stage_ref, sem.at[0])

    @pl.when(t == 0)
    def _():
        copy(t).start()

    copy(t).wait()

    def cast(r, carry):
        sl = pl.ds(pl.multiple_of(r * CAST_ROWS, CAST_ROWS), CAST_ROWS)
        wb_ref[sl, :] = stage_ref[sl, :].astype(_bf16)
        return carry

    lax.fori_loop(0, rows // CAST_ROWS, cast, 0)

    @pl.when(t + 1 < n_blocks)
    def _():
        copy(t + 1).start()


def _stream_weight_block(w_hbm, stage_ref, wb_ref, sems, layer, t, i, n_blocks, col0_of, body):
    n_steps = sems.shape[0]
    k, cols = stage_ref.shape
    ck = k // n_steps

    def chunk(c):
        return pl.ds(c * ck if isinstance(c, int) else pl.multiple_of(c * ck, ck), ck)

    def copy(tt, c):
        src = w_hbm.at[layer, chunk(c), pl.ds(pl.multiple_of(col0_of(tt), cols), cols)]
        return pltpu.make_async_copy(src, stage_ref.at[chunk(c)], sems.at[c])

    @pl.when((t == 0) & (i == 0))
    def _():
        for c in range(n_steps):
            copy(t, c).start()
        for c in range(n_steps):
            copy(t, c).wait()
            wb_ref[0, chunk(c), :] = stage_ref[chunk(c), :].astype(_bf16)

        @pl.when(n_blocks > 1)
        def _():
            for c in range(n_steps):
                copy(t + 1, c).start()

    @pl.when(t + 1 < n_blocks)
    def _():
        copy(t + 1, i).wait()

    for slot in range(2):
        @pl.when(t % 2 == slot)
        def _(slot=slot):
            wb_ref[1 - slot, chunk(i), :] = stage_ref[chunk(i), :].astype(_bf16)
            body(wb_ref.at[slot])

    @pl.when(t + 2 < n_blocks)
    def _():
        copy(t + 2, i).start()


def _stream_row_block(m, target, k):
    for bm in range(min(m, target) // 16 * 16, 0, -16):
        if m % bm == 0 and k % (m // bm) == 0 and (k // (m // bm)) % 16 == 0:
            return bm
    raise ValueError(f"no streaming row block for m={m}, k={k}")


def _stream_scratch(k, bn, n_steps):
    assert k % n_steps == 0 and (k // n_steps) % 16 == 0
    return [pltpu.VMEM((k, bn), _f32), pltpu.VMEM((2, k, bn), _bf16),
            pltpu.SemaphoreType.DMA((n_steps,))]


def _column_slices(bn):
    step = MXU_COLS if bn % MXU_COLS == 0 else bn
    return [slice(c0, c0 + step) for c0 in range(0, bn, step)]


def _weight_scratch(bk, bn):
    assert bk % CAST_ROWS == 0
    return [pltpu.VMEM((bk, bn), _f32), pltpu.VMEM((bk, bn), _bf16), pltpu.SemaphoreType.DMA((1,))]


def _ada_down_kernel(s_ref, wd_ref, t_ref):
    t_ref[0] = jnp.dot(s_ref[...].astype(_bf16), wd_ref[0].astype(_bf16),
                       preferred_element_type=_f32)


def _ada_up_kernel(t_ref, wu_ref, b_ref, m_ref):
    m_ref[0] = jnp.dot(t_ref[0].astype(_bf16), wu_ref[0].astype(_bf16),
                       preferred_element_type=_f32) + b_ref[0]


def _ada_mod(s, w_down, w_up, b):
    depth, d, rank = w_down.shape
    n = w_up.shape[2]
    t = pl.pallas_call(
        _ada_down_kernel,
        out_shape=jax.ShapeDtypeStruct((depth, ADA_ROWS, rank), _f32),
        grid=(depth,),
        in_specs=[pl.BlockSpec((ADA_ROWS, d), lambda l: (0, 0)),
                  pl.BlockSpec((1, d, rank), lambda l: (l, 0, 0))],
        out_specs=pl.BlockSpec((1, ADA_ROWS, rank), lambda l: (l, 0, 0)),
        compiler_params=_params(("arbitrary",)),
        name="ada_down",
    )(s, w_down)
    bn = _block(n, 4096, 128)
    return pl.pallas_call(
        _ada_up_kernel,
        out_shape=jax.ShapeDtypeStruct((depth, ADA_ROWS, n), _f32),
        grid=(depth, n // bn),
        in_specs=[pl.BlockSpec((1, ADA_ROWS, rank), lambda l, j: (l, 0, 0)),
                  pl.BlockSpec((1, rank, bn), lambda l, j: (l, 0, j)),
                  pl.BlockSpec((1, 1, bn), lambda l, j: (l, 0, j))],
        out_specs=pl.BlockSpec((1, ADA_ROWS, bn), lambda l, j: (l, 0, j)),
        compiler_params=_params(("arbitrary", "arbitrary")),
        name="ada_up",
    )(t, w_up, b.reshape(depth, 1, n))


def _row_select(mod_ref, row0, rows, n_lat):
    ridx = row0 + lax.broadcasted_iota(jnp.int32, (rows, 1), 0)
    return jnp.where(ridx < n_lat, mod_ref[0:1, :], mod_ref[1:2, :])


def _norm_mod_kernel(x_ref, w_ref, sh_ref, sc_ref, o_ref, gain_ref, shift_ref, rstd_ref, *, n_lat):
    rows, d = x_ref.shape
    g_rows = NORM_GROUP
    w = w_ref[...]
    for r in range(2):
        gain_ref[r * g_rows:(r + 1) * g_rows, :] = jnp.broadcast_to(
            w * (1.0 + sc_ref[r:r + 1, :]), (g_rows, d))
        shift_ref[r * g_rows:(r + 1) * g_rows, :] = jnp.broadcast_to(sh_ref[r:r + 1, :], (g_rows, d))

    def group(g):
        return pl.ds(pl.multiple_of(g * g_rows, g_rows), g_rows)

    def stats(g, carry):
        x = x_ref[group(g), :]
        rstd = lax.rsqrt(jnp.mean(x * x, axis=-1, keepdims=True) + EPS)
        rstd_ref[group(g), :] = jnp.broadcast_to(rstd, (g_rows, NORM_LANES))
        return carry

    lax.fori_loop(0, rows // g_rows, stats, 0, unroll=8)

    steps = rows // (g_rows * NORM_BUNDLE)
    lat_steps = jnp.clip(n_lat - pl.program_id(0) * rows, 0, rows) // (g_rows * NORM_BUNDLE)
    reps = NORM_TILE // NORM_LANES
    for ct in range(d // NORM_TILE):
        cols = slice(ct * NORM_TILE, (ct + 1) * NORM_TILE)
        for mod_row0, lo, hi in ((0, 0, lat_steps), (g_rows, lat_steps, steps)):
            gain = gain_ref[mod_row0:mod_row0 + g_rows, cols]
            shift = shift_ref[mod_row0:mod_row0 + g_rows, cols]

            def apply(s, carry, gain=gain, shift=shift, cols=cols):
                for b in range(NORM_BUNDLE):
                    rows_g = group(s * NORM_BUNDLE + b)
                    r = rstd_ref[rows_g, :]
                    rr = jnp.concatenate([r] * reps, axis=1)
                    o_ref[rows_g, cols] = ((x_ref[rows_g, cols] * rr) * gain
                                           + shift).astype(o_ref.dtype)
                return carry

            lax.fori_loop(lo, hi, apply, 0)


def _norm_scratch(br, d):
    assert d % NORM_TILE == 0 and br % (NORM_GROUP * NORM_BUNDLE) == 0
    return [pltpu.VMEM((2 * NORM_GROUP, d), _f32), pltpu.VMEM((2 * NORM_GROUP, d), _f32),
            pltpu.VMEM((br, NORM_LANES), _f32)]


def _norm_mod_first_kernel(x_ref, ctx_ref, w_ref, sh_ref, sc_ref, h_ref, o_ref, gain_ref,
                           shift_ref, rstd_ref, *, n_lat):
    in_lat = pl.program_id(0) * h_ref.shape[0] < n_lat

    @pl.when(in_lat)
    def _():
        h_ref[...] = x_ref[...]

    @pl.when(jnp.logical_not(in_lat))
    def _():
        h_ref[...] = ctx_ref[...]

    _norm_mod_kernel(h_ref, w_ref, sh_ref, sc_ref, o_ref, gain_ref, shift_ref, rstd_ref,
                     n_lat=n_lat)


def _norm_mod_first(x, ctx, w, sh, sc):
    n_lat, d = x.shape
    n_ctx = ctx.shape[0]
    m = n_lat + n_ctx
    br = _block(n_ctx, 512, 8 * NORM_GROUP)
    assert n_lat % br == 0
    lat_blocks = n_lat // br
    row_spec = pl.BlockSpec((br, d), lambda i: (i, 0))
    vec_spec = pl.BlockSpec((ADA_ROWS, d), lambda i: (0, 0))
    return pl.pallas_call(
        functools.partial(_norm_mod_first_kernel, n_lat=n_lat),
        out_shape=(jax.ShapeDtypeStruct((m, d), _f32), jax.ShapeDtypeStruct((m, d), _bf16)),
        grid=(m // br,),
        in_specs=[pl.BlockSpec((br, d), lambda i: (jnp.minimum(i, lat_blocks - 1), 0)),
                  pl.BlockSpec((br, d), lambda i: (jnp.maximum(i - lat_blocks, 0), 0)),
                  pl.BlockSpec((1, d), lambda i: (0, 0)), vec_spec, vec_spec],
        out_specs=(row_spec, row_spec),
        scratch_shapes=_norm_scratch(br, d),
        compiler_params=_params(("arbitrary",)),
        name="norm_mod_first",
    )(x, ctx, w.reshape(1, d), sh, sc)


def _norm_mod(h, w, sh, sc, n_lat):
    m, d = h.shape
    br = _block(m, 512, 8 * NORM_GROUP)
    assert n_lat % NORM_GROUP == 0
    return pl.pallas_call(
        functools.partial(_norm_mod_kernel, n_lat=n_lat),
        out_shape=jax.ShapeDtypeStruct((m, d), _bf16),
        grid=(m // br,),
        in_specs=[pl.BlockSpec((br, d), lambda i: (i, 0)),
                  pl.BlockSpec((1, d), lambda i: (0, 0)),
                  pl.BlockSpec((ADA_ROWS, d), lambda i: (0, 0)),
                  pl.BlockSpec((ADA_ROWS, d), lambda i: (0, 0))],
        out_specs=pl.BlockSpec((br, d), lambda i: (i, 0)),
        scratch_shapes=_norm_scratch(br, d),
        compiler_params=_params(("arbitrary",)),
        name="norm_mod",
    )(h, w.reshape(1, d), sh, sc)


def _final_norm_kernel(x_ref, w_ref, o_ref):
    x = x_ref[...]
    o_ref[...] = x * lax.rsqrt(jnp.mean(x * x, axis=-1, keepdims=True) + EPS) * w_ref[...]


def _final_norm(h, w, n_lat):
    d = h.shape[1]
    br = _block(n_lat, 256, 8)
    return pl.pallas_call(
        _final_norm_kernel,
        out_shape=jax.ShapeDtypeStruct((n_lat, d), _f32),
        grid=(n_lat // br,),
        in_specs=[pl.BlockSpec((br, d), lambda i: (i, 0)),
                  pl.BlockSpec((1, d), lambda i: (0, 0))],
        out_specs=pl.BlockSpec((br, d), lambda i: (i, 0)),
        compiler_params=_params(("arbitrary",)),
        name="final_norm",
    )(h, w.reshape(1, d))


def _in_proj_rope_kernel(a_ref, w_hbm, cos_ref, sin_ref, o_ref, stage_ref, wb_ref, sem, *,
                         layer, n_q_blocks, head_dim):
    j = pl.program_id(0)
    bn = wb_ref.shape[2]
    half = head_dim // 2

    def body(wb):
        scale = jnp.where(j < n_q_blocks, head_dim ** -0.5, 1.0).astype(_f32)
        cos = cos_ref[...] * scale
        sin = sin_ref[...] * scale
        for c0 in range(0, bn, head_dim):
            y = jnp.dot(a_ref[...], wb[:, c0:c0 + head_dim], preferred_element_type=_f32)
            t1 = y[:, :half]
            t2 = y[:, half:]
            o_ref[:, c0:c0 + half] = (t1 * cos - t2 * sin).astype(o_ref.dtype)
            o_ref[:, c0 + half:c0 + head_dim] = (t1 * sin + t2 * cos).astype(o_ref.dtype)

    _stream_weight_block(w_hbm, stage_ref, wb_ref, sem, layer, j, pl.program_id(1),
                         pl.num_programs(0), lambda t: t * bn, body)


def _in_proj_plain_kernel(a_ref, w_hbm, o_ref, stage_ref, wb_ref, sem, *, layer, col0):
    bn = wb_ref.shape[2]

    def body(wb):
        for cols in _column_slices(bn):
            o_ref[:, cols] = jnp.dot(a_ref[...], wb[:, cols],
                                     preferred_element_type=_f32).astype(o_ref.dtype)

    _stream_weight_block(w_hbm, stage_ref, wb_ref, sem, layer, pl.program_id(0),
                         pl.program_id(1), pl.num_programs(0), lambda t: col0 + t * bn, body)


def _in_proj(a, w, layer, cos, sin, ret_width, head_dim):
    m, k = a.shape
    n = w.shape[2]
    bm = _stream_row_block(m, 1056, k)
    bn = _block(ret_width, 1024, head_dim)
    n_rope = 2 * ret_width
    assert (n - n_rope) % bn == 0
    half = head_dim // 2
    a_spec = pl.BlockSpec((bm, k), lambda j, i: (i, 0))
    o_spec = pl.BlockSpec((bm, bn), lambda j, i: (i, j))
    qk = pl.pallas_call(
        functools.partial(_in_proj_rope_kernel, layer=layer, n_q_blocks=ret_width // bn,
                          head_dim=head_dim),
        out_shape=jax.ShapeDtypeStruct((m, n_rope), _bf16),
        grid=(n_rope // bn, m // bm),
        in_specs=[a_spec, pl.BlockSpec(memory_space=pl.ANY),
                  pl.BlockSpec((bm, half), lambda j, i: (i, 0)),
                  pl.BlockSpec((bm, half), lambda j, i: (i, 0))],
        out_specs=o_spec,
        scratch_shapes=_stream_scratch(k, bn, m // bm),
        compiler_params=_params(("arbitrary", "arbitrary")),
        name="in_proj_qk",
    )(a, w, cos, sin)
    rest = pl.pallas_call(
        functools.partial(_in_proj_plain_kernel, layer=layer, col0=n_rope),
        out_shape=jax.ShapeDtypeStruct((m, n - n_rope), _bf16),
        grid=((n - n_rope) // bn, m // bm),
        in_specs=[a_spec, pl.BlockSpec(memory_space=pl.ANY)],
        out_specs=o_spec,
        scratch_shapes=_stream_scratch(k, bn, m // bm),
        compiler_params=_params(("arbitrary", "arbitrary")),
        name="in_proj_rest",
    )(a, w)
    return qk, rest


def _retention_kernel(lg_ref, q_ref, k_ref, v_ref, g_ref, w_ref, o_ref, snap_ref, s_ref,
                      decay_ref, *, chunk, n_ctx_chunks, unroll):
    head = pl.program_id(0)
    n_chunks = q_ref.shape[0] // chunk
    n_lat_chunks = n_chunks - n_ctx_chunks
    lg_f = lg_ref[0, head]
    lg_b = lg_ref[1, head]

    ii = lax.broadcasted_iota(jnp.int32, (chunk, chunk), 0)
    jj = lax.broadcasted_iota(jnp.int32, (chunk, chunk), 1)
    diff = (ii - jj).astype(_f32)
    decay_ref[...] = jnp.where(diff >= 0.0, jnp.exp(lg_f * jnp.maximum(diff, 0.0)),
                               jnp.exp(lg_b * jnp.maximum(-diff, 0.0)))
    pos = lax.broadcasted_iota(jnp.int32, (chunk, 1), 0).astype(_f32)
    chunk_len = jnp.full((1, 1), float(chunk), _f32)
    q_decay_f = jnp.exp(lg_f * (pos + 1.0))
    k_decay_f = jnp.exp(lg_f * (chunk - 1.0 - pos))
    gamma_f = jnp.exp(lg_f * chunk_len)
    q_decay_b = jnp.exp(lg_b * (chunk - pos))
    k_decay_b = jnp.exp(lg_b * pos)
    gamma_b = jnp.exp(lg_b * chunk_len)

    def scaled(t, d):
        return (t.astype(_f32) * d).astype(_bf16)

    def kv_outer(kc, vc, d):
        return lax.dot_general(scaled(kc, d), vc, (((0,), (0,)), ((), ())),
                               preferred_element_type=_f32)

    def chunk_rows(c):
        return pl.ds(pl.multiple_of(c * chunk, chunk), chunk)

    s_ref[...] = jnp.zeros_like(s_ref)

    def fwd(t, carry):
        c = jnp.where(t < n_ctx_chunks, t + n_lat_chunks, t - n_ctx_chunks)
        rows = chunk_rows(c)
        snap_ref[c] = s_ref[...].astype(_bf16)
        s_ref[...] = gamma_f * s_ref[...] + kv_outer(k_ref[rows, :], v_ref[rows, :], k_decay_f)
        return carry

    lax.fori_loop(0, n_chunks, fwd, 0, unroll=unroll)

    s_ref[...] = jnp.zeros_like(s_ref)
    w = w_ref[...]

    def bwd(t, carry):
        c = n_chunks - 1 - t
        rows = chunk_rows(c)
        qc, kc, vc = q_ref[rows, :], k_ref[rows, :], v_ref[rows, :]
        s = lax.dot_general(qc, kc, (((1,), (1,)), ((), ())), preferred_element_type=_f32)
        o = jnp.dot((s * decay_ref[...]).astype(_bf16), vc, preferred_element_type=_f32)
        o += q_decay_f * jnp.dot(qc, snap_ref[c], preferred_element_type=_f32)
        o += q_decay_b * jnp.dot(qc, s_ref[...].astype(_bf16), preferred_element_type=_f32)
        s_ref[...] = gamma_b * s_ref[...] + kv_outer(kc, vc, k_decay_b)
        r = o * lax.rsqrt(jnp.mean(o * o, axis=-1, keepdims=True) + EPS) * w
        g = g_ref[rows, :].astype(_f32)
        o_ref[rows, :] = (g * (1.0 / (1.0 + jnp.exp(-g))) * r).astype(o_ref.dtype)
        return carry

    lax.fori_loop(0, n_chunks, bwd, 0, unroll=unroll)


def _retention(qk, rest, lg, norm_w, n_ctx, heads, head_dim):
    m = qk.shape[0]
    chunk = max(c for c in RET_CHUNKS if n_ctx % c == 0 and m % c == 0)
    n_chunks = m // chunk
    ret_width = heads * head_dim
    blk = lambda part: pl.BlockSpec((m, head_dim), lambda h: (0, part * heads + h))
    return pl.pallas_call(
        functools.partial(_retention_kernel, chunk=chunk, n_ctx_chunks=n_ctx // chunk,
                          unroll=max(u for u in (3, 2, 1) if n_chunks % u == 0)),
        out_shape=jax.ShapeDtypeStruct((m, ret_width), _bf16),
        grid=(heads,),
        in_specs=[pl.BlockSpec(memory_space=pltpu.SMEM),
                  blk(0), blk(1), blk(0), blk(1),
                  pl.BlockSpec((1, head_dim), lambda h: (0, h))],
        out_specs=pl.BlockSpec((m, head_dim), lambda h: (0, h)),
        scratch_shapes=[pltpu.VMEM((n_chunks, head_dim, head_dim), _bf16),
                        pltpu.VMEM((head_dim, head_dim), _f32),
                        pltpu.VMEM((chunk, chunk), _f32)],
        compiler_params=_params(("arbitrary",)),
        name="retention",
    )(lg, qk, qk, rest, rest, norm_w.reshape(1, ret_width))


def _conv_kernel(bg_ref, cg_ref, xc_ref, w_ref, o_ref, *, n_lat, rows_per_step):
    m = o_ref.shape[0]
    r = rows_per_step
    w0, w1, w2 = w_ref[0:1, :], w_ref[1:2, :], w_ref[2:3, :]
    ridx = lax.broadcasted_iota(jnp.int32, (r, 1), 0)

    def u_rows(start, size):
        rows = pl.ds(pl.multiple_of(start, 16), size)
        return cg_ref[rows, :].astype(_f32) * xc_ref[rows, :].astype(_f32)

    def step(i, prev_last):
        r0 = i * r
        u = u_rows(r0, r)
        nxt_first = u_rows(jnp.minimum(r0 + r, m - 16), 16)[0:1, :]
        prev_row = jnp.where((r0 == 0) | (r0 == n_lat), 0.0, prev_last)
        next_row = jnp.where((r0 + r == n_lat) | (r0 + r == m), 0.0, nxt_first)
        up = jnp.where(ridx == 0, prev_row, pltpu.roll(u, 1, axis=0))
        dn = jnp.where(ridx == r - 1, next_row, pltpu.roll(u, r - 1, axis=0))
        conv = up * w0 + u * w1 + dn * w2
        rows = pl.ds(pl.multiple_of(r0, 16), r)
        o_ref[rows, :] = (bg_ref[rows, :].astype(_f32) * conv).astype(o_ref.dtype)
        return u[r - 1:r, :]

    lax.fori_loop(0, m // r, step, jnp.zeros((1, o_ref.shape[1]), _f32))


def _gated_conv(y, conv_w_t, n_lat, ret_width, conv_width):
    m = y.shape[0]
    bc = _block(conv_width, 256, 128)
    nb = conv_width // bc
    base = 2 * ret_width // bc
    r = _block(m - n_lat, 128, 16)
    assert n_lat % r == 0
    blk = lambda part: pl.BlockSpec((m, bc), lambda j: (0, base + part * nb + j))
    return pl.pallas_call(
        functools.partial(_conv_kernel, n_lat=n_lat, rows_per_step=r),
        out_shape=jax.ShapeDtypeStruct((m, conv_width), _bf16),
        grid=(nb,),
        in_specs=[blk(0), blk(1), blk(2), pl.BlockSpec((3, bc), lambda j: (0, j))],
        out_specs=pl.BlockSpec((m, bc), lambda j: (0, j)),
        compiler_params=_params(("arbitrary",)),
        name="gated_conv",
    )(y, y, y, conv_w_t)


def _out_proj_kernel(r_ref, c_ref, w_hbm, h_ref, g_ref, o_ref, stage_ref, wb_ref, sem, *,
                     layer, n_lat):
    rows, kr = r_ref.shape
    bn = wb_ref.shape[2]

    def body(wb):
        gate = _row_select(g_ref, pl.program_id(1) * rows, rows, n_lat)
        for cols in _column_slices(bn):
            y = (jnp.dot(r_ref[...], wb[0:kr, cols], preferred_element_type=_f32)
                 + jnp.dot(c_ref[...], wb[kr:, cols], preferred_element_type=_f32))
            o_ref[:, cols] = h_ref[:, cols] + gate[:, cols] * y

    _stream_weight_block(w_hbm, stage_ref, wb_ref, sem, layer, pl.program_id(0),
                         pl.program_id(1), pl.num_programs(0), lambda t: t * bn, body)


def _out_proj(ret, conv, w, layer, h, gate, n_lat, m):
    kr = ret.shape[1]
    kc = conv.shape[1]
    n = w.shape[2]
    bm = _stream_row_block(m, 528, kr + kc)
    bn = _block(n, 1024, 128)
    return pl.pallas_call(
        functools.partial(_out_proj_kernel, layer=layer, n_lat=n_lat),
        out_shape=jax.ShapeDtypeStruct((m, n), _f32),
        grid=(n // bn, m // bm),
        in_specs=[pl.BlockSpec((bm, kr), lambda j, i: (i, 0)),
                  pl.BlockSpec((bm, kc), lambda j, i: (i, 0)),
                  pl.BlockSpec(memory_space=pl.ANY),
                  pl.BlockSpec((bm, bn), lambda j, i: (i, j)),
                  pl.BlockSpec((ADA_ROWS, bn), lambda j, i: (0, j))],
        out_specs=pl.BlockSpec((bm, bn), lambda j, i: (i, j)),
        scratch_shapes=_stream_scratch(kr + kc, bn, m // bm),
        compiler_params=_params(("arbitrary", "arbitrary")),
        name="out_proj",
    )(ret, conv, w, h, gate)


def _mlp_up_kernel(a_ref, w_hbm, o_ref, stage_ref, wb_ref, sem, *, layer):
    bn = wb_ref.shape[2]

    def body(wb):
        for cols in _column_slices(bn):
            z = jnp.maximum(jnp.dot(a_ref[...], wb[:, cols], preferred_element_type=_f32), 0.0)
            o_ref[:, cols] = (z * z).astype(o_ref.dtype)

    _stream_weight_block(w_hbm, stage_ref, wb_ref, sem, layer, pl.program_id(0),
                         pl.program_id(1), pl.num_programs(0), lambda t: t * bn, body)


def _mlp_up(a, w, layer):
    m, k = a.shape
    n = w.shape[2]
    bm = _stream_row_block(m, 1056, k)
    bn = _block(n, 1024, 128)
    return pl.pallas_call(
        functools.partial(_mlp_up_kernel, layer=layer),
        out_shape=jax.ShapeDtypeStruct((m, n), _bf16),
        grid=(n // bn, m // bm),
        in_specs=[pl.BlockSpec((bm, k), lambda j, i: (i, 0)),
                  pl.BlockSpec(memory_space=pl.ANY)],
        out_specs=pl.BlockSpec((bm, bn), lambda j, i: (i, j)),
        scratch_shapes=_stream_scratch(k, bn, m // bm),
        compiler_params=_params(("arbitrary", "arbitrary")),
        name="mlp_up",
    )(a, w)


def _mlp_down_kernel(z_ref, w_hbm, h_ref, g_ref, o_ref, stage_ref, wb_ref, sem, acc_ref, *,
                     layer, n_lat, nk):
    j, kk, i = pl.program_id(0), pl.program_id(1), pl.program_id(2)
    bm = z_ref.shape[0]
    bk, bn = wb_ref.shape

    @pl.when(i == 0)
    def _():
        _fetch_weight_block(w_hbm, stage_ref, wb_ref, sem, layer, j * nk + kk,
                            pl.num_programs(0) * nk, lambda t: ((t % nk) * bk, (t // nk) * bn))

    rows = pl.ds(pl.multiple_of(i * bm, bm), bm)

    def partial_product():
        return jnp.dot(z_ref[...], wb_ref[...], preferred_element_type=_f32)

    def finish(total):
        gate = _row_select(g_ref, i * bm, bm, n_lat)
        o_ref[...] = h_ref[...] + gate * total

    if nk == 1:
        finish(partial_product())
        return

    @pl.when(kk == 0)
    def _():
        acc_ref[rows, :] = partial_product()

    if nk > 2:
        @pl.when((kk > 0) & (kk < nk - 1))
        def _():
            acc_ref[rows, :] += partial_product()

    @pl.when(kk == nk - 1)
    def _():
        finish(acc_ref[rows, :] + partial_product())


def _mlp_down(z, w, layer, h, gate, n_lat):
    m, k = z.shape
    n = w.shape[2]
    bm = _block(m, 1056, 16)
    bn = _block(n, 512, 128)
    bk = _block(k, 4096, CAST_ROWS)
    nk = k // bk
    res_map = lambda j, kk, i: (jnp.where(kk == nk - 1, i, 0), j)
    return pl.pallas_call(
        functools.partial(_mlp_down_kernel, layer=layer, n_lat=n_lat, nk=nk),
        out_shape=jax.ShapeDtypeStruct((m, n), _f32),
        grid=(n // bn, nk, m // bm),
        in_specs=[pl.BlockSpec((bm, bk), lambda j, kk, i: (i, kk)),
                  pl.BlockSpec(memory_space=pl.ANY),
                  pl.BlockSpec((bm, bn), res_map),
                  pl.BlockSpec((ADA_ROWS, bn), lambda j, kk, i: (0, j))],
        out_specs=pl.BlockSpec((bm, bn), res_map),
        scratch_shapes=_weight_scratch(bk, bn) + [pltpu.VMEM((m, bn), _f32)],
        compiler_params=_params(("arbitrary", "arbitrary", "arbitrary")),
        name="mlp_down",
    )(z, w, h, gate)


def _rope_tables(n_lat, n_ctx, head_dim):
    rows = n_lat // GRID_W
    row = jnp.broadcast_to(jnp.arange(rows, dtype=_f32)[:, None], (rows, GRID_W)).reshape(-1)
    col = jnp.broadcast_to(jnp.arange(GRID_W, dtype=_f32)[None, :], (rows, GRID_W)).reshape(-1)
    nf = head_dim // 4
    inv = ROPE_THETA ** (-jnp.arange(nf, dtype=_f32) / nf)
    ang = jnp.concatenate([row[:, None] * inv, col[:, None] * inv], axis=-1)
    cos = jnp.concatenate([jnp.cos(ang), jnp.ones((n_ctx, 2 * nf), _f32)], axis=0)
    sin = jnp.concatenate([jnp.sin(ang), jnp.zeros((n_ctx, 2 * nf), _f32)], axis=0)
    return cos, sin


def kernel(x, c, ctx, c_ctx, ada_w_down, ada_w_up, ada_b, norm1_w, norm2_w, w_in, ret_decay_fwd,
           ret_decay_bwd, ret_norm_w, conv_w, w_out, mlp_w1, mlp_w2, final_norm_w):
    batch, n_lat, d = x.shape
    n_ctx = ctx.shape[1]
    depth = w_in.shape[0]
    heads = ret_decay_fwd.shape[1]
    ret_width = ret_norm_w.shape[1]
    conv_width = conv_w.shape[1]
    head_dim = ret_width // heads
    assert batch == 1 and c.shape[0] == 1
    assert n_ctx % CHUNK == 0 and n_lat % CHUNK == 0 and n_lat % GRID_W == 0
    assert w_in.shape[2] == 4 * ret_width + 3 * conv_width and ret_width + conv_width == d

    s = jnp.zeros((ADA_ROWS, d), _f32).at[0].set(jax.nn.silu(c[0])).at[1].set(jax.nn.silu(c_ctx))
    mods = _ada_mod(s, ada_w_down, ada_w_up, ada_b)
    cos, sin = _rope_tables(n_lat, n_ctx, head_dim)
    lg = jnp.stack([jax.nn.log_sigmoid(ret_decay_fwd.astype(_f32)),
                    jax.nn.log_sigmoid(ret_decay_bwd.astype(_f32))], axis=1)

    h = None
    for l in range(depth):
        sh1, sc1, g1, sh2, sc2, g2 = [mods[l, :, j * d:(j + 1) * d] for j in range(N_MOD)]
        if l == 0:
            h, a = _norm_mod_first(x[0], ctx[0], norm1_w[l], sh1, sc1)
        else:
            a = _norm_mod(h, norm1_w[l], sh1, sc1, n_lat)
        qk, rest = _in_proj(a, w_in, l, cos, sin, ret_width, head_dim)
        ret = _retention(qk, rest, lg[l], ret_norm_w[l], n_ctx, heads, head_dim)
        conv = _gated_conv(rest, conv_w[l].T, n_lat, ret_width, conv_width)
        m_out = n_lat if l == depth - 1 else n_lat + n_ctx
        h = _out_proj(ret, conv, w_out, l, h, g1, n_lat, m_out)
        a = _norm_mod(h, norm2_w[l], sh2, sc2, n_lat)
        z = _mlp_up(a, mlp_w1, l)
        h = _mlp_down(z, mlp_w2, l, h, g2, n_lat)
    return _final_norm(h, final_norm_w, n_lat)[None]
```

```python
import functools

import jax
import jax.numpy as jnp
from jax import lax
from jax.experimental import pallas as pl
from jax.experimental.pallas import tpu as pltpu

GRID_W = 64
CHUNK = 128
ROPE_THETA = 10000.0
EPS = 1e-6
N_MOD = 6
ADA_ROWS = 8
NORM_GROUP = 16
NORM_LANES = 128
NORM_TILE = 512
NORM_BUNDLE = 8
CAST_ROWS = 256
RET_CHUNKS = (256, 128)
MXU_COLS = 256
V7X_VMEM_LIMIT_BYTES = 60 * 1024 * 1024

_bf16 = jnp.bfloat16
_f32 = jnp.float32


def _block(total, target, align):
    best = None
    for d in range(align, min(total, target) + 1, align):
        if total % d == 0:
            best = d
    if best is None:
        raise ValueError(f"no block for {total} (target {target}, align {align})")
    return best


def _params(sem):
    return pltpu.CompilerParams(dimension_semantics=sem, vmem_limit_bytes=V7X_VMEM_LIMIT_BYTES)


def _fetch_weight_block(w_hbm, stage_ref, wb_ref, sem, layer, t, n_blocks, origin):
    rows, cols = stage_ref.shape

    def aligned(v, a):
        return v if isinstance(v, int) else pl.multiple_of(v, a)

    def copy(tt):
        r0, c0 = origin(tt)
        src = w_hbm.at[layer, pl.ds(aligned(r0, rows), rows), pl.ds(aligned(c0, cols), cols)]
        return pltpu.make_async_copy(src, stage_ref, sem.at[0])

    @pl.when(t == 0)
    def _():
        copy(t).start()

    copy(t).wait()

    def cast(r, carry):
        sl = pl.ds(pl.multiple_of(r * CAST_ROWS, CAST_ROWS), CAST_ROWS)
        wb_ref[sl, :] = stage_ref[sl, :].astype(_bf16)
        return carry

    lax.fori_loop(0, rows // CAST_ROWS, cast, 0)

    @pl.when(t + 1 < n_blocks)
    def _():
        copy(t + 1).start(priority=1)


def _column_slices(bn):
    step = MXU_COLS if bn % MXU_COLS == 0 else bn
    return [slice(c0, c0 + step) for c0 in range(0, bn, step)]


def _weight_scratch(bk, bn):
    assert bk % CAST_ROWS == 0
    return [pltpu.VMEM((bk, bn), _f32), pltpu.VMEM((bk, bn), _bf16), pltpu.SemaphoreType.DMA((1,))]


def _ada_down_kernel(s_ref, wd_ref, t_ref):
    t_ref[0] = jnp.dot(s_ref[...].astype(_bf16), wd_ref[0].astype(_bf16),
                       preferred_element_type=_f32)


def _ada_up_kernel(t_ref, wu_ref, b_ref, m_ref):
    m_ref[0] = jnp.dot(t_ref[0].astype(_bf16), wu_ref[0].astype(_bf16),
                       preferred_element_type=_f32) + b_ref[0]


def _ada_mod(s, w_down, w_up, b):
    depth, d, rank = w_down.shape
    n = w_up.shape[2]
    t = pl.pallas_call(
        _ada_down_kernel,
        out_shape=jax.ShapeDtypeStruct((depth, ADA_ROWS, rank), _f32),
        grid=(depth,),
        in_specs=[pl.BlockSpec((ADA_ROWS, d), lambda l: (0, 0)),
                  pl.BlockSpec((1, d, rank), lambda l: (l, 0, 0))],
        out_specs=pl.BlockSpec((1, ADA_ROWS, rank), lambda l: (l, 0, 0)),
        compiler_params=_params(("arbitrary",)),
        name="ada_down",
    )(s, w_down)
    bn = _block(n, 4096, 128)
    return pl.pallas_call(
        _ada_up_kernel,
        out_shape=jax.ShapeDtypeStruct((depth, ADA_ROWS, n), _f32),
        grid=(depth, n // bn),
        in_specs=[pl.BlockSpec((1, ADA_ROWS, rank), lambda l, j: (l, 0, 0)),
                  pl.BlockSpec((1, rank, bn), lambda l, j: (l, 0, j)),
                  pl.BlockSpec((1, 1, bn), lambda l, j: (l, 0, j))],
        out_specs=pl.BlockSpec((1, ADA_ROWS, bn), lambda l, j: (l, 0, j)),
        compiler_params=_params(("arbitrary", "arbitrary")),
        name="ada_up",
    )(t, w_up, b.reshape(depth, 1, n))


def _row_select(mod_ref, row0, rows, n_lat):
    ridx = row0 + lax.broadcasted_iota(jnp.int32, (rows, 1), 0)
    return jnp.where(ridx < n_lat, mod_ref[0:1, :], mod_ref[1:2, :])


def _norm_mod_kernel(x_ref, w_ref, sh_ref, sc_ref, o_ref, gain_ref, shift_ref, rstd_ref, *, n_lat):
    rows, d = x_ref.shape
    g_rows = NORM_GROUP
    w = w_ref[...]
    for r in range(2):
        gain_ref[r * g_rows:(r + 1) * g_rows, :] = jnp.broadcast_to(
            w * (1.0 + sc_ref[r:r + 1, :]), (g_rows, d))
        shift_ref[r * g_rows:(r + 1) * g_rows, :] = jnp.broadcast_to(sh_ref[r:r + 1, :], (g_rows, d))

    def group(g):
        return pl.ds(pl.multiple_of(g * g_rows, g_rows), g_rows)

    def stats(g, carry):
        x = x_ref[group(g), :]
        rstd = lax.rsqrt(jnp.mean(x * x, axis=-1, keepdims=True) + EPS)
        rstd_ref[group(g), :] = jnp.broadcast_to(rstd, (g_rows, NORM_LANES))
        return carry

    lax.fori_loop(0, rows // g_rows, stats, 0, unroll=8)

    steps = rows // (g_rows * NORM_BUNDLE)
    lat_steps = jnp.clip(n_lat - pl.program_id(0) * rows, 0, rows) // (g_rows * NORM_BUNDLE)
    reps = NORM_TILE // NORM_LANES
    for ct in range(d // NORM_TILE):
        cols = slice(ct * NORM_TILE, (ct + 1) * NORM_TILE)
        for mod_row0, lo, hi in ((0, 0, lat_steps), (g_rows, lat_steps, steps)):
            gain = gain_ref[mod_row0:mod_row0 + g_rows, cols]
            shift = shift_ref[mod_row0:mod_row0 + g_rows, cols]

            def apply(s, carry, gain=gain, shift=shift, cols=cols):
                for b in range(NORM_BUNDLE):
                    rows_g = group(s * NORM_BUNDLE + b)
                    r = rstd_ref[rows_g, :]
                    rr = jnp.concatenate([r] * reps, axis=1)
                    o_ref[rows_g, cols] = ((x_ref[rows_g, cols] * rr) * gain
                                           + shift).astype(o_ref.dtype)
                return carry

            lax.fori_loop(lo, hi, apply, 0)


def _norm_scratch(br, d):
    assert d % NORM_TILE == 0 and br % (NORM_GROUP * NORM_BUNDLE) == 0
    return [pltpu.VMEM((2 * NORM_GROUP, d), _f32), pltpu.VMEM((2 * NORM_GROUP, d), _f32),
            pltpu.VMEM((br, NORM_LANES), _f32)]


def _norm_mod_first_kernel(x_ref, ctx_ref, w_ref, sh_ref, sc_ref, h_ref, o_ref, gain_ref,
                           shift_ref, rstd_ref, *, n_lat):
    in_lat = pl.program_id(0) * h_ref.shape[0] < n_lat

    @pl.when(in_lat)
    def _():
        h_ref[...] = x_ref[...]

    @pl.when(jnp.logical_not(in_lat))
    def _():
        h_ref[...] = ctx_ref[...]

    _norm_mod_kernel(h_ref, w_ref, sh_ref, sc_ref, o_ref, gain_ref, shift_ref, rstd_ref,
                     n_lat=n_lat)


def _norm_mod_first(x, ctx, w, sh, sc):
    n_lat, d = x.shape
    n_ctx = ctx.shape[0]
    m = n_lat + n_ctx
    br = _block(n_ctx, 512, 8 * NORM_GROUP)
    assert n_lat % br == 0
    lat_blocks = n_lat // br
    row_spec = pl.BlockSpec((br, d), lambda i: (i, 0))
    vec_spec = pl.BlockSpec((ADA_ROWS, d), lambda i: (0, 0))
    return pl.pallas_call(
        functools.partial(_norm_mod_first_kernel, n_lat=n_lat),
        out_shape=(jax.ShapeDtypeStruct((m, d), _f32), jax.ShapeDtypeStruct((m, d), _bf16)),
        grid=(m // br,),
        in_specs=[pl.BlockSpec((br, d), lambda i: (jnp.minimum(i, lat_blocks - 1), 0)),
                  pl.BlockSpec((br, d), lambda i: (jnp.maximum(i - lat_blocks, 0), 0)),
                  pl.BlockSpec((1, d), lambda i: (0, 0)), vec_spec, vec_spec],
        out_specs=(row_spec, row_spec),
        scratch_shapes=_norm_scratch(br, d),
        compiler_params=_params(("arbitrary",)),
        name="norm_mod_first",
    )(x, ctx, w.reshape(1, d), sh, sc)


def _norm_mod(h, w, sh, sc, n_lat):
    m, d = h.shape
    br = _block(m, 512, 8 * NORM_GROUP)
    assert n_lat % NORM_GROUP == 0
    return pl.pallas_call(
        functools.partial(_norm_mod_kernel, n_lat=n_lat),
        out_shape=jax.ShapeDtypeStruct((m, d), _bf16),
        grid=(m // br,),
        in_specs=[pl.BlockSpec((br, d), lambda i: (i, 0)),
                  pl.BlockSpec((1, d), lambda i: (0, 0)),
                  pl.BlockSpec((ADA_ROWS, d), lambda i: (0, 0)),
                  pl.BlockSpec((ADA_ROWS, d), lambda i: (0, 0))],
        out_specs=pl.BlockSpec((br, d), lambda i: (i, 0)),
        scratch_shapes=_norm_scratch(br, d),
        compiler_params=_params(("arbitrary",)),
        name="norm_mod",
    )(h, w.reshape(1, d), sh, sc)


def _final_norm_kernel(x_ref, w_ref, o_ref):
    x = x_ref[...]
    o_ref[...] = x * lax.rsqrt(jnp.mean(x * x, axis=-1, keepdims=True) + EPS) * w_ref[...]


def _final_norm(h, w, n_lat):
    d = h.shape[1]
    br = _block(n_lat, 256, 8)
    return pl.pallas_call(
        _final_norm_kernel,
        out_shape=jax.ShapeDtypeStruct((n_lat, d), _f32),
        grid=(n_lat // br,),
        in_specs=[pl.BlockSpec((br, d), lambda i: (i, 0)),
                  pl.BlockSpec((1, d), lambda i: (0, 0))],
        out_specs=pl.BlockSpec((br, d), lambda i: (i, 0)),
        compiler_params=_params(("arbitrary",)),
        name="final_norm",
    )(h, w.reshape(1, d))


def _in_proj_rope_kernel(a_ref, w_hbm, cos_ref, sin_ref, o_ref, stage_ref, wb_ref, sem, *,
                         layer, n_q_blocks, head_dim):
    j = pl.program_id(0)
    bn = wb_ref.shape[1]

    @pl.when(pl.program_id(1) == 0)
    def _():
        _fetch_weight_block(w_hbm, stage_ref, wb_ref, sem, layer, j, pl.num_programs(0),
                            lambda t: (0, t * bn))

    half = head_dim // 2
    scale = jnp.where(j < n_q_blocks, head_dim ** -0.5, 1.0).astype(_f32)
    cos = cos_ref[...] * scale
    sin = sin_ref[...] * scale
    for c0 in range(0, bn, head_dim):
        y = jnp.dot(a_ref[...], wb_ref[:, c0:c0 + head_dim], preferred_element_type=_f32)
        t1 = y[:, :half]
        t2 = y[:, half:]
        o_ref[:, c0:c0 + half] = (t1 * cos - t2 * sin).astype(o_ref.dtype)
        o_ref[:, c0 + half:c0 + head_dim] = (t1 * sin + t2 * cos).astype(o_ref.dtype)


def _in_proj_plain_kernel(a_ref, w_hbm, o_ref, stage_ref, wb_ref, sem, *, layer, col0):
    bn = wb_ref.shape[1]

    @pl.when(pl.program_id(1) == 0)
    def _():
        _fetch_weight_block(w_hbm, stage_ref, wb_ref, sem, layer, pl.program_id(0),
                            pl.num_programs(0), lambda t: (0, col0 + t * bn))

    for cols in _column_slices(bn):
        o_ref[:, cols] = jnp.dot(a_ref[...], wb_ref[:, cols],
                                 preferred_element_type=_f32).astype(o_ref.dtype)


def _in_proj(a, w, layer, cos, sin, ret_width, head_dim):
    m, k = a.shape
    n = w.shape[2]
    bm = _block(m, 1056, 16)
    bn = _block(ret_width, 1024, head_dim)
    n_rope = 2 * ret_width
    assert (n - n_rope) % bn == 0
    half = head_dim // 2
    a_spec = pl.BlockSpec((bm, k), lambda j, i: (i, 0))
    o_spec = pl.BlockSpec((bm, bn), lambda j, i: (i, j))
    qk = pl.pallas_call(
        functools.partial(_in_proj_rope_kernel, layer=layer, n_q_blocks=ret_width // bn,
                          head_dim=head_dim),
        out_shape=jax.ShapeDtypeStruct((m, n_rope), _bf16),
        grid=(n_rope // bn, m // bm),
        in_specs=[a_spec, pl.BlockSpec(memory_space=pl.ANY),
                  pl.BlockSpec((bm, half), lambda j, i: (i, 0)),
                  pl.BlockSpec((bm, half), lambda j, i: (i, 0))],
        out_specs=o_spec,
        scratch_shapes=_weight_scratch(k, bn),
        compiler_params=_params(("arbitrary", "arbitrary")),
        name="in_proj_qk",
    )(a, w, cos, sin)
    rest = pl.pallas_call(
        functools.partial(_in_proj_plain_kernel, layer=layer, col0=n_rope),
        out_shape=jax.ShapeDtypeStruct((m, n - n_rope), _bf16),
        grid=((n - n_rope) // bn, m // bm),
        in_specs=[a_spec, pl.BlockSpec(memory_space=pl.ANY)],
        out_specs=o_spec,
        scratch_shapes=_weight_scratch(k, bn),
        compiler_params=_params(("arbitrary", "arbitrary")),
        name="in_proj_rest",
    )(a, w)
    return qk, rest


def _retention_kernel(lg_ref, q_ref, k_ref, v_ref, g_ref, w_ref, o_ref, snap_ref, s_ref,
                      decay_ref, *, chunk, n_ctx_chunks, unroll):
    head = pl.program_id(0)
    n_chunks = q_ref.shape[0] // chunk
    n_lat_chunks = n_chunks - n_ctx_chunks
    lg_f = lg_ref[0, head]
    lg_b = lg_ref[1, head]

    ii = lax.broadcasted_iota(jnp.int32, (chunk, chunk), 0)
    jj = lax.broadcasted_iota(jnp.int32, (chunk, chunk), 1)
    diff = (ii - jj).astype(_f32)
    decay_ref[...] = jnp.where(diff >= 0.0, jnp.exp(lg_f * jnp.maximum(diff, 0.0)),
                               jnp.exp(lg_b * jnp.maximum(-diff, 0.0)))
    pos = lax.broadcasted_iota(jnp.int32, (chunk, 1), 0).astype(_f32)
    chunk_len = jnp.full((1, 1), float(chunk), _f32)
    q_decay_f = jnp.exp(lg_f * (pos + 1.0))
    k_decay_f = jnp.exp(lg_f * (chunk - 1.0 - pos))
    gamma_f = jnp.exp(lg_f * chunk_len)
    q_decay_b = jnp.exp(lg_b * (chunk - pos))
    k_decay_b = jnp.exp(lg_b * pos)
    gamma_b = jnp.exp(lg_b * chunk_len)

    def scaled(t, d):
        return (t.astype(_f32) * d).astype(_bf16)

    def kv_outer(kc, vc, d):
        return lax.dot_general(scaled(kc, d), vc, (((0,), (0,)), ((), ())),
                               preferred_element_type=_f32)

    def chunk_rows(c):
        return pl.ds(pl.multiple_of(c * chunk, chunk), chunk)

    s_ref[...] = jnp.zeros_like(s_ref)

    def fwd(t, carry):
        c = jnp.where(t < n_ctx_chunks, t + n_lat_chunks, t - n_ctx_chunks)
        rows = chunk_rows(c)
        snap_ref[c] = s_ref[...].astype(_bf16)
        s_ref[...] = gamma_f * s_ref[...] + kv_outer(k_ref[rows, :], v_ref[rows, :], k_decay_f)
        return carry

    lax.fori_loop(0, n_chunks, fwd, 0, unroll=unroll)

    s_ref[...] = jnp.zeros_like(s_ref)
    w = w_ref[...]

    def bwd(t, carry):
        c = n_chunks - 1 - t
        rows = chunk_rows(c)
        qc, kc, vc = q_ref[rows, :], k_ref[rows, :], v_ref[rows, :]
        s = lax.dot_general(qc, kc, (((1,), (1,)), ((), ())), preferred_element_type=_f32)
        o = jnp.dot((s * decay_ref[...]).astype(_bf16), vc, preferred_element_type=_f32)
        o += q_decay_f * jnp.dot(qc, snap_ref[c], preferred_element_type=_f32)
        o += q_decay_b * jnp.dot(qc, s_ref[...].astype(_bf16), preferred_element_type=_f32)
        s_ref[...] = gamma_b * s_ref[...] + kv_outer(kc, vc, k_decay_b)
        r = o * lax.rsqrt(jnp.mean(o * o, axis=-1, keepdims=True) + EPS) * w
        g = g_ref[rows, :].astype(_f32)
        o_ref[rows, :] = (g * (1.0 / (1.0 + jnp.exp(-g))) * r).astype(o_ref.dtype)
        return carry

    lax.fori_loop(0, n_chunks, bwd, 0, unroll=unroll)


def _retention(qk, rest, lg, norm_w, n_ctx, heads, head_dim):
    m = qk.shape[0]
    chunk = max(c for c in RET_CHUNKS if n_ctx % c == 0 and m % c == 0)
    n_chunks = m // chunk
    ret_width = heads * head_dim
    blk = lambda part: pl.BlockSpec((m, head_dim), lambda h: (0, part * heads + h))
    return pl.pallas_call(
        functools.partial(_retention_kernel, chunk=chunk, n_ctx_chunks=n_ctx // chunk,
                          unroll=max(u for u in (3, 2, 1) if n_chunks % u == 0)),
        out_shape=jax.ShapeDtypeStruct((m, ret_width), _bf16),
        grid=(heads,),
        in_specs=[pl.BlockSpec(memory_space=pltpu.SMEM),
                  blk(0), blk(1), blk(0), blk(1),
                  pl.BlockSpec((1, head_dim), lambda h: (0, h))],
        out_specs=pl.BlockSpec((m, head_dim), lambda h: (0, h)),
        scratch_shapes=[pltpu.VMEM((n_chunks, head_dim, head_dim), _bf16),
                        pltpu.VMEM((head_dim, head_dim), _f32),
                        pltpu.VMEM((chunk, chunk), _f32)],
        compiler_params=_params(("arbitrary",)),
        name="retention",
    )(lg, qk, qk, rest, rest, norm_w.reshape(1, ret_width))


def _conv_kernel(bg_ref, cg_ref, xc_ref, w_ref, o_ref, *, n_lat, rows_per_step):
    m = o_ref.shape[0]
    r = rows_per_step
    w0, w1, w2 = w_ref[0:1, :], w_ref[1:2, :], w_ref[2:3, :]
    ridx = lax.broadcasted_iota(jnp.int32, (r, 1), 0)

    def u_rows(start, size):
        rows = pl.ds(pl.multiple_of(start, 16), size)
        return cg_ref[rows, :].astype(_f32) * xc_ref[rows, :].astype(_f32)

    def step(i, prev_last):
        r0 = i * r
        u = u_rows(r0, r)
        nxt_first = u_rows(jnp.minimum(r0 + r, m - 16), 16)[0:1, :]
        prev_row = jnp.where((r0 == 0) | (r0 == n_lat), 0.0, prev_last)
        next_row = jnp.where((r0 + r == n_lat) | (r0 + r == m), 0.0, nxt_first)
        up = jnp.where(ridx == 0, prev_row, pltpu.roll(u, 1, axis=0))
        dn = jnp.where(ridx == r - 1, next_row, pltpu.roll(u, r - 1, axis=0))
        conv = up * w0 + u * w1 + dn * w2
        rows = pl.ds(pl.multiple_of(r0, 16), r)
        o_ref[rows, :] = (bg_ref[rows, :].astype(_f32) * conv).astype(o_ref.dtype)
        return u[r - 1:r, :]

    lax.fori_loop(0, m // r, step, jnp.zeros((1, o_ref.shape[1]), _f32))


def _gated_conv(y, conv_w_t, n_lat, ret_width, conv_width):
    m = y.shape[0]
    bc = _block(conv_width, 256, 128)
    nb = conv_width // bc
    base = 2 * ret_width // bc
    r = _block(m - n_lat, 128, 16)
    assert n_lat % r == 0
    blk = lambda part: pl.BlockSpec((m, bc), lambda j: (0, base + part * nb + j))
    return pl.pallas_call(
        functools.partial(_conv_kernel, n_lat=n_lat, rows_per_step=r),
        out_shape=jax.ShapeDtypeStruct((m, conv_width), _bf16),
        grid=(nb,),
        in_specs=[blk(0), blk(1), blk(2), pl.BlockSpec((3, bc), lambda j: (0, j))],
        out_specs=pl.BlockSpec((m, bc), lambda j: (0, j)),
        compiler_params=_params(("arbitrary",)),
        name="gated_conv",
    )(y, y, y, conv_w_t)


def _out_proj_kernel(r_ref, c_ref, w_hbm, h_ref, g_ref, o_ref, stage_ref, wb_ref, sem, *,
                     layer, n_lat):
    rows, kr = r_ref.shape
    bn = wb_ref.shape[1]

    @pl.when(pl.program_id(1) == 0)
    def _():
        _fetch_weight_block(w_hbm, stage_ref, wb_ref, sem, layer, pl.program_id(0),
                            pl.num_programs(0), lambda t: (0, t * bn))

    gate = _row_select(g_ref, pl.program_id(1) * rows, rows, n_lat)
    for cols in _column_slices(bn):
        y = (jnp.dot(r_ref[...], wb_ref[0:kr, cols], preferred_element_type=_f32)
             + jnp.dot(c_ref[...], wb_ref[kr:, cols], preferred_element_type=_f32))
        o_ref[:, cols] = h_ref[:, cols] + gate[:, cols] * y


def _out_proj(ret, conv, w, layer, h, gate, n_lat, m):
    kr = ret.shape[1]
    kc = conv.shape[1]
    n = w.shape[2]
    bm = _block(m, 704, 16)
    bn = _block(n, 1024, 128)
    return pl.pallas_call(
        functools.partial(_out_proj_kernel, layer=layer, n_lat=n_lat),
        out_shape=jax.ShapeDtypeStruct((m, n), _f32),
        grid=(n // bn, m // bm),
        in_specs=[pl.BlockSpec((bm, kr), lambda j, i: (i, 0)),
                  pl.BlockSpec((bm, kc), lambda j, i: (i, 0)),
                  pl.BlockSpec(memory_space=pl.ANY),
                  pl.BlockSpec((bm, bn), lambda j, i: (i, j)),
                  pl.BlockSpec((ADA_ROWS, bn), lambda j, i: (0, j))],
        out_specs=pl.BlockSpec((bm, bn), lambda j, i: (i, j)),
        scratch_shapes=_weight_scratch(kr + kc, bn),
        compiler_params=_params(("arbitrary", "arbitrary")),
        name="out_proj",
    )(ret, conv, w, h, gate)


def _mlp_up_kernel(a_ref, w_hbm, o_ref, stage_ref, wb_ref, sem, *, layer):
    bn = wb_ref.shape[1]

    @pl.when(pl.program_id(1) == 0)
    def _():
        _fetch_weight_block(w_hbm, stage_ref, wb_ref, sem, layer, pl.program_id(0),
                            pl.num_programs(0), lambda t: (0, t * bn))

    for cols in _column_slices(bn):
        z = jnp.maximum(jnp.dot(a_ref[...], wb_ref[:, cols], preferred_element_type=_f32), 0.0)
        o_ref[:, cols] = (z * z).astype(o_ref.dtype)


def _mlp_up(a, w, layer):
    m, k = a.shape
    n = w.shape[2]
    bm = _block(m, 1408, 16)
    bn = _block(n, 1024, 128)
    return pl.pallas_call(
        functools.partial(_mlp_up_kernel, layer=layer),
        out_shape=jax.ShapeDtypeStruct((m, n), _bf16),
        grid=(n // bn, m // bm),
        in_specs=[pl.BlockSpec((bm, k), lambda j, i: (i, 0)),
                  pl.BlockSpec(memory_space=pl.ANY)],
        out_specs=pl.BlockSpec((bm, bn), lambda j, i: (i, j)),
        scratch_shapes=_weight_scratch(k, bn),
        compiler_params=_params(("arbitrary", "arbitrary")),
        name="mlp_up",
    )(a, w)


def _mlp_down_kernel(z_ref, w_hbm, h_ref, g_ref, o_ref, stage_ref, wb_ref, sem, acc_ref, *,
                     layer, n_lat, nk):
    j, kk, i = pl.program_id(0), pl.program_id(1), pl.program_id(2)
    bm = z_ref.shape[0]
    bk, bn = wb_ref.shape

    @pl.when(i == 0)
    def _():
        _fetch_weight_block(w_hbm, stage_ref, wb_ref, sem, layer, j * nk + kk,
                            pl.num_programs(0) * nk, lambda t: ((t % nk) * bk, (t // nk) * bn))

    rows = pl.ds(pl.multiple_of(i * bm, bm), bm)

    def partial_product():
        return jnp.dot(z_ref[...], wb_ref[...], preferred_element_type=_f32)

    def finish(total):
        gate = _row_select(g_ref, i * bm, bm, n_lat)
        o_ref[...] = h_ref[...] + gate * total

    if nk == 1:
        finish(partial_product())
        return

    @pl.when(kk == 0)
    def _():
        acc_ref[rows, :] = partial_product()

    if nk > 2:
        @pl.when((kk > 0) & (kk < nk - 1))
        def _():
            acc_ref[rows, :] += partial_product()

    @pl.when(kk == nk - 1)
    def _():
        finish(acc_ref[rows, :] + partial_product())


def _mlp_down(z, w, layer, h, gate, n_lat):
    m, k = z.shape
    n = w.shape[2]
    bm = _block(m, 1056, 16)
    bn = _block(n, 512, 128)
    bk = _block(k, 4096, CAST_ROWS)
    nk = k // bk
    res_map = lambda j, kk, i: (jnp.where(kk == nk - 1, i, 0), j)
    return pl.pallas_call(
        functools.partial(_mlp_down_kernel, layer=layer, n_lat=n_lat, nk=nk),
        out_shape=jax.ShapeDtypeStruct((m, n), _f32),
        grid=(n // bn, nk, m // bm),
        in_specs=[pl.BlockSpec((bm, bk), lambda j, kk, i: (i, kk)),
                  pl.BlockSpec(memory_space=pl.ANY),
                  pl.BlockSpec((bm, bn), res_map),
                  pl.BlockSpec((ADA_ROWS, bn), lambda j, kk, i: (0, j))],
        out_specs=pl.BlockSpec((bm, bn), res_map),
        scratch_shapes=_weight_scratch(bk, bn) + [pltpu.VMEM((m, bn), _f32)],
        compiler_params=_params(("arbitrary", "arbitrary", "arbitrary")),
        name="mlp_down",
    )(z, w, h, gate)


def _rope_tables(n_lat, n_ctx, head_dim):
    rows = n_lat // GRID_W
    row = jnp.broadcast_to(jnp.arange(rows, dtype=_f32)[:, None], (rows, GRID_W)).reshape(-1)
    col = jnp.broadcast_to(jnp.arange(GRID_W, dtype=_f32)[None, :], (rows, GRID_W)).reshape(-1)
    nf = head_dim // 4
    inv = ROPE_THETA ** (-jnp.arange(nf, dtype=_f32) / nf)
    ang = jnp.concatenate([row[:, None] * inv, col[:, None] * inv], axis=-1)
    cos = jnp.concatenate([jnp.cos(ang), jnp.ones((n_ctx, 2 * nf), _f32)], axis=0)
    sin = jnp.concatenate([jnp.sin(ang), jnp.zeros((n_ctx, 2 * nf), _f32)], axis=0)
    return cos, sin


def kernel(x, c, ctx, c_ctx, ada_w_down, ada_w_up, ada_b, norm1_w, norm2_w, w_in, ret_decay_fwd,
           ret_decay_bwd, ret_norm_w, conv_w, w_out, mlp_w1, mlp_w2, final_norm_w):
    batch, n_lat, d = x.shape
    n_ctx = ctx.shape[1]
    depth = w_in.shape[0]
    heads = ret_decay_fwd.shape[1]
    ret_width = ret_norm_w.shape[1]
    conv_width = conv_w.shape[1]
    head_dim = ret_width // heads
    assert batch == 1 and c.shape[0] == 1
    assert n_ctx % CHUNK == 0 and n_lat % CHUNK == 0 and n_lat % GRID_W == 0
    assert w_in.shape[2] == 4 * ret_width + 3 * conv_width and ret_width + conv_width == d

    s = jnp.zeros((ADA_ROWS, d), _f32).at[0].set(jax.nn.silu(c[0])).at[1].set(jax.nn.silu(c_ctx))
    mods = _ada_mod(s, ada_w_down, ada_w_up, ada_b)
    cos, sin = _rope_tables(n_lat, n_ctx, head_dim)
    lg = jnp.stack([jax.nn.log_sigmoid(ret_decay_fwd.astype(_f32)),
                    jax.nn.log_sigmoid(ret_decay_bwd.astype(_f32))], axis=1)

    h = None
    for l in range(depth):
        sh1, sc1, g1, sh2, sc2, g2 = [mods[l, :, j * d:(j + 1) * d] for j in range(N_MOD)]
        if l == 0:
            h, a = _norm_mod_first(x[0], ctx[0], norm1_w[l], sh1, sc1)
        else:
            a = _norm_mod(h, norm1_w[l], sh1, sc1, n_lat)
        qk, rest = _in_proj(a, w_in, l, cos, sin, ret_width, head_dim)
        ret = _retention(qk, rest, lg[l], ret_norm_w[l], n_ctx, heads, head_dim)
        conv = _gated_conv(rest, conv_w[l].T, n_lat, ret_width, conv_width)
        m_out = n_lat if l == depth - 1 else n_lat + n_ctx
        h = _out_proj(ret, conv, w_out, l, h, g1, n_lat, m_out)
        a = _norm_mod(h, norm2_w[l], sh2, sc2, n_lat)
        z = _mlp_up(a, mlp_w1, l)
        h = _mlp_down(z, mlp_w2, l, h, g2, n_lat)
    return _final_norm(h, final_norm_w, n_lat)[None]
```
